```python
import jax, jax.numpy as jnp
from jax import lax
import numpy as np

D_MODEL = 1024
BATCH = 4
SEQ = 4096
DEPTH = 2

CHUNK = 64
Q_BLOCK = 128
N_MIXERS = 2
N_LRU_LAYERS = (DEPTH + 1) // N_MIXERS
N_MLA_LAYERS = DEPTH // N_MIXERS

LRU_WIDTH = D_MODEL
LRU_HEADS = 4
LRU_BLOCK = LRU_WIDTH // LRU_HEADS
LRU_CONV = 4
LRU_C = 8.0

MLA_HEADS = 16
Q_LORA = 768
KV_LORA = 256
QK_NOPE = 64
QK_ROPE = 32
V_DIM = 64
QK_DIM = QK_NOPE + QK_ROPE
ROPE_THETA = 10000.0

D_FF = 2816
FFN_CONV = 3

EPS = 1e-6

kernel_name = "hybrid_rglru_mla_convffn_sandwich"


def rms_norm(x, g):
    x32 = x.astype(jnp.float32)
    y = x32 * lax.rsqrt(jnp.mean(x32 * x32, axis=-1, keepdims=True) + EPS)
    return (y * g.astype(jnp.float32)).astype(x.dtype)


def causal_dwconv(x, w, b):
    k, c = w.shape
    y = lax.conv_general_dilated(x, w[:, None, :], window_strides=(1,), padding=[(k - 1, 0)],
                                 dimension_numbers=("NWC", "WIO", "NWC"), feature_group_count=c)
    return y + b


def lru_combine(left, right):
    a1, b1 = left
    a2, b2 = right
    return a1 * a2, a2 * b1 + b2


def rglru_mixer(h, w_in, b_in, conv_w, conv_b, w_r, b_r, w_i, b_i, lam, w_out, b_out):
    B, S, _ = h.shape
    u = h @ w_in + b_in
    gate, xb = jnp.split(u, 2, axis=-1)
    xb = causal_dwconv(xb, conv_w, conv_b)
    xh = xb.reshape(B, S, LRU_HEADS, LRU_BLOCK)
    r = jax.nn.sigmoid(jnp.einsum('bshi,hij->bshj', xh, w_r) + b_r).reshape(B, S, LRU_WIDTH)
    ig = jax.nn.sigmoid(jnp.einsum('bshi,hij->bshj', xh, w_i) + b_i).reshape(B, S, LRU_WIDTH)
    f32 = jnp.float32
    log_a = -LRU_C * r.astype(f32) * jax.nn.softplus(-lam.astype(f32))
    a = jnp.exp(log_a)
    inp = jnp.sqrt(-jnp.expm1(2.0 * log_a)) * (ig * xb).astype(f32)
    _, hs = lax.associative_scan(lru_combine, (a, inp), axis=1)
    y = jax.nn.gelu(gate, approximate=True) * hs.astype(h.dtype)
    return y @ w_out + b_out


def apply_rope(x, cos, sin):
    x1, x2 = jnp.split(x, 2, axis=-1)
    return jnp.concatenate([x1 * cos - x2 * sin, x1 * sin + x2 * cos], axis=-1)


def chunk_causal_attention(q, k, v):
    B, S, H, Dk = q.shape
    scale = Dk ** -0.5
    outs = []
    for s0 in range(0, S, Q_BLOCK):
        end = s0 + Q_BLOCK
        kb = k[:, :end]
        vb = v[:, :end]
        s = jnp.einsum('bqhd,bkhd->bhqk', q[:, s0:end], kb).astype(jnp.float32) * scale
        q_chunk = jnp.arange(s0, end) // CHUNK
        k_chunk = jnp.arange(end) // CHUNK
        mask = k_chunk[None, :] <= q_chunk[:, None]
        s = jnp.where(mask, s, -jnp.inf)
        p = jax.nn.softmax(s, axis=-1).astype(vb.dtype)
        outs.append(jnp.einsum('bhqk,bkhd->bqhd', p, vb))
    return jnp.concatenate(outs, axis=1)


def mla_mixer(h, cos, sin, w_in, q_norm_g, w_qb, kv_norm_g, w_kvb, w_out):
    B, S, _ = h.shape
    c = h @ w_in
    q_c, kv_c, k_pe = jnp.split(c, [Q_LORA, Q_LORA + KV_LORA], axis=-1)
    q = (rms_norm(q_c, q_norm_g) @ w_qb).reshape(B, S, MLA_HEADS, QK_DIM)
    q_nope, q_pe = jnp.split(q, [QK_NOPE], axis=-1)
    q_pe = apply_rope(q_pe, cos[:, :, None, :], sin[:, :, None, :])
    k_pe = apply_rope(k_pe, cos, sin)
    kv = (rms_norm(kv_c, kv_norm_g) @ w_kvb).reshape(B, S, MLA_HEADS, QK_NOPE + V_DIM)
    k_nope, v = jnp.split(kv, [QK_NOPE], axis=-1)
    q = jnp.concatenate([q_nope, q_pe], axis=-1)
    k = jnp.concatenate([k_nope, jnp.broadcast_to(k_pe[:, :, None, :], (B, S, MLA_HEADS, QK_ROPE))], axis=-1)
    o = chunk_causal_attention(q, k, v)
    return o.reshape(B, S, MLA_HEADS * V_DIM) @ w_out


def conv_ffn(h, w_up, conv_w, conv_b, w_down):
    u = causal_dwconv(h @ w_up, conv_w, conv_b)
    g, val = jnp.split(u, 2, axis=-1)
    return (jax.nn.gelu(g, approximate=True) * val) @ w_down


def setup_inputs(seed: int = 0) -> dict:
    key = jax.random.key(seed)
    ks = iter(jax.random.split(key, 40))
    f32 = jnp.float32

    def nrm(shape, scale):
        return jax.random.normal(next(ks), shape, f32) * scale

    def gain(shape):
        return 1.0 + nrm(shape, 0.05)

    NA, NB = N_LRU_LAYERS, N_MLA_LAYERS
    x = jax.random.normal(next(ks), (BATCH, SEQ, D_MODEL), f32)
    offset = jax.random.randint(next(ks), (BATCH, 1), 0, 4096, dtype=jnp.int32)
    positions = (offset + jnp.arange(SEQ, dtype=jnp.int32)[None, :]).astype(jnp.int32)

    u = jax.random.uniform(next(ks), (NA, LRU_WIDTH), f32, minval=0.9, maxval=0.999)
    s = u ** (1.0 / LRU_C)
    lru_lambda = jnp.log(s) - jnp.log1p(-s)

    return {
        "x": x,
        "positions": positions,
        "mix_pre_g": gain((DEPTH, D_MODEL)),
        "mix_post_g": gain((DEPTH, D_MODEL)),
        "ffn_pre_g": gain((DEPTH, D_MODEL)),
        "ffn_post_g": gain((DEPTH, D_MODEL)),
        "lru_w_in": nrm((NA, D_MODEL, 2 * LRU_WIDTH), D_MODEL ** -0.5),
        "lru_b_in": nrm((NA, 2 * LRU_WIDTH), 0.01),
        "lru_conv_w": nrm((NA, LRU_CONV, LRU_WIDTH), LRU_CONV ** -0.5),
        "lru_conv_b": nrm((NA, LRU_WIDTH), 0.01),
        "lru_w_r": nrm((NA, LRU_HEADS, LRU_BLOCK, LRU_BLOCK), LRU_BLOCK ** -0.5),
        "lru_b_r": nrm((NA, LRU_HEADS, LRU_BLOCK), 0.01),
        "lru_w_i": nrm((NA, LRU_HEADS, LRU_BLOCK, LRU_BLOCK), LRU_BLOCK ** -0.5),
        "lru_b_i": nrm((NA, LRU_HEADS, LRU_BLOCK), 0.01),
        "lru_lambda": lru_lambda,
        "lru_w_out": nrm((NA, LRU_WIDTH, D_MODEL), LRU_WIDTH ** -0.5),
        "lru_b_out": nrm((NA, D_MODEL), 0.01),
        "mla_w_in": nrm((NB, D_MODEL, Q_LORA + KV_LORA + QK_ROPE), D_MODEL ** -0.5),
        "mla_q_norm_g": gain((NB, Q_LORA)),
        "mla_w_qb": nrm((NB, Q_LORA, MLA_HEADS * QK_DIM), Q_LORA ** -0.5),
        "mla_kv_norm_g": gain((NB, KV_LORA)),
        "mla_w_kvb": nrm((NB, KV_LORA, MLA_HEADS * (QK_NOPE + V_DIM)), KV_LORA ** -0.5),
        "mla_w_out": nrm((NB, MLA_HEADS * V_DIM, D_MODEL), (MLA_HEADS * V_DIM) ** -0.5),
        "ffn_w_up": nrm((DEPTH, D_MODEL, 2 * D_FF), D_MODEL ** -0.5),
        "ffn_conv_w": nrm((DEPTH, FFN_CONV, 2 * D_FF), FFN_CONV ** -0.5),
        "ffn_conv_b": nrm((DEPTH, 2 * D_FF), 0.01),
        "ffn_w_down": nrm((DEPTH, D_FF, D_MODEL), D_FF ** -0.5),
    }


def reference(x, positions, mix_pre_g, mix_post_g, ffn_pre_g, ffn_post_g,
              lru_w_in, lru_b_in, lru_conv_w, lru_conv_b, lru_w_r, lru_b_r, lru_w_i, lru_b_i,
              lru_lambda, lru_w_out, lru_b_out,
              mla_w_in, mla_q_norm_g, mla_w_qb, mla_kv_norm_g, mla_w_kvb, mla_w_out,
              ffn_w_up, ffn_conv_w, ffn_conv_b, ffn_w_down):
    inv_freq = 1.0 / (ROPE_THETA ** (jnp.arange(0, QK_ROPE, 2, dtype=jnp.float32) / QK_ROPE))
    ang = positions.astype(jnp.float32)[..., None] * inv_freq
    cos = jnp.cos(ang).astype(x.dtype)
    sin = jnp.sin(ang).astype(x.dtype)

    for i in range(DEPTH):
        j = i // N_MIXERS
        h = rms_norm(x, mix_pre_g[i])
        if i % N_MIXERS == 0:
            m = rglru_mixer(h, lru_w_in[j], lru_b_in[j], lru_conv_w[j], lru_conv_b[j],
                            lru_w_r[j], lru_b_r[j], lru_w_i[j], lru_b_i[j], lru_lambda[j],
                            lru_w_out[j], lru_b_out[j])
        else:
            m = mla_mixer(h, cos, sin, mla_w_in[j], mla_q_norm_g[j], mla_w_qb[j],
                          mla_kv_norm_g[j], mla_w_kvb[j], mla_w_out[j])
        x = x + rms_norm(m, mix_post_g[i])
        h = rms_norm(x, ffn_pre_g[i])
        x = x + rms_norm(conv_ffn(h, ffn_w_up[i], ffn_conv_w[i], ffn_conv_b[i], ffn_w_down[i]), ffn_post_g[i])
    return x
```

```python
import functools
import math

import jax
import jax.numpy as jnp
from jax import lax
from jax.experimental import pallas as pl
from jax.experimental.pallas import tpu as pltpu

D_MODEL = 1024
BATCH = 4
SEQ = 4096
CHUNK = 64

LRU_WIDTH = 1024
LRU_HEADS = 4
LRU_BLOCK = LRU_WIDTH // LRU_HEADS
LRU_CONV = 4
LRU_C = 8.0

MLA_HEADS = 16
Q_LORA = 768
KV_LORA = 256
QK_NOPE = 64
QK_ROPE = 32
ROPE_HALF = QK_ROPE // 2
V_DIM = 64
QK_DIM = QK_NOPE + QK_ROPE
ROPE_THETA = 10000.0

D_FF = 2816
FFN_CONV = 3
EPS = 1e-6

LANES = 128
SUBLANES = 8
BF16_ROWS = 16

HEAD_LANES = LANES
NOPE_A = 48
X1_OFF = 0
NOPE_A_OFF = ROPE_HALF
X2_OFF = HEAD_LANES // 2
NOPE_B_OFF = X2_OFF + ROPE_HALF
PE_LANES = HEAD_LANES

TM = 512
FFN_CHUNK = 256
N_FFN_CHUNKS = D_FF // FFN_CHUNK
TQ = 512
HALO = BF16_ROWS

F32 = jnp.float32
BF16 = jnp.bfloat16
MASK_VALUE = -1e30


def _rms(x, g):
    ms = jnp.mean(x * x, axis=-1, keepdims=True)
    return x * lax.rsqrt(ms + EPS) * g


def _gelu_tanh(x):
    c = math.sqrt(2.0 / math.pi)
    return 0.5 * x * (1.0 + jnp.tanh(c * (x + 0.044715 * (x * x * x))))


def _const_spec(shape):
    nd = len(shape)
    return pl.BlockSpec(shape, lambda b, j: (0,) * nd, pipeline_mode=pl.Buffered(1))


def _rope_kernel(pos_ref, freq_ref, cos_ref, sin_ref):
    ang = pos_ref[...].astype(F32) * freq_ref[...]
    cos_ref[...] = jnp.cos(ang)
    sin_ref[...] = jnp.sin(ang)


def _rope_tables(positions):
    rows = BATCH * SEQ * ROPE_HALF // LANES
    pos_rep = jnp.repeat(positions.reshape(-1), ROPE_HALF).reshape(rows, LANES)
    inv_freq = 1.0 / (ROPE_THETA ** (jnp.arange(0, QK_ROPE, 2, dtype=F32) / QK_ROPE))
    freq_row = jnp.tile(inv_freq, LANES // ROPE_HALF).reshape(1, LANES)
    cos_d, sin_d = pl.pallas_call(
        _rope_kernel,
        out_shape=(jax.ShapeDtypeStruct((rows, LANES), F32),) * 2,
        name="rope_tables",
    )(pos_rep, freq_row)
    cos = cos_d.reshape(BATCH, SEQ, ROPE_HALF)
    sin = sin_d.reshape(BATCH, SEQ, ROPE_HALF)
    ones = jnp.ones((BATCH, SEQ, NOPE_A), F32)
    zeros = jnp.zeros((BATCH, SEQ, NOPE_A), F32)
    ctab = jnp.concatenate([cos, ones, cos, ones], axis=-1)
    stab = jnp.concatenate([-sin, zeros, sin, zeros], axis=-1)
    return ctab, stab


def _lru_kernel(x_ref, pre_g, post_g, w_in, b_in, conv_w, conv_b, w_ri, b_ri, lam, w_out, b_out,
                o_ref, xb_scr, a_scr, b_scr, h_scr):
    j = pl.program_id(1)
    x = x_ref[0]
    h = _rms(x, pre_g[...]).astype(BF16)
    u = jnp.dot(h, w_in[...], preferred_element_type=F32) + b_in[...]
    gate = u[:, :LRU_WIDTH]

    @pl.when(j == 0)
    def _():
        xb_scr[0:SUBLANES, :] = jnp.zeros((SUBLANES, LRU_WIDTH), F32)
        h_scr[...] = jnp.zeros((1, LRU_WIDTH), F32)

    @pl.when(j > 0)
    def _():
        xb_scr[0:SUBLANES, :] = xb_scr[TM:TM + SUBLANES, :]

    xb_scr[SUBLANES:, :] = u[:, LRU_WIDTH:]
    xe = xb_scr[...]
    xc = xe * conv_w[LRU_CONV - 1:LRU_CONV, :] + conv_b[...]
    for k in range(1, LRU_CONV):
        xc = xc + pltpu.roll(xe, k, 0) * conv_w[LRU_CONV - 1 - k:LRU_CONV - k, :]
    xc = xc[SUBLANES:, :]

    sp = lam[...]
    neg_c_softplus = -LRU_C * (jnp.maximum(-sp, 0.0) + jnp.log(1.0 + jnp.exp(-jnp.abs(sp))))
    for hd in range(LRU_HEADS):
        sl = slice(hd * LRU_BLOCK, (hd + 1) * LRU_BLOCK)
        xh = xc[:, sl]
        z = jnp.dot(xh.astype(BF16), w_ri[hd], preferred_element_type=F32) + b_ri[hd]
        r = jax.nn.sigmoid(z[:, :LRU_BLOCK])
        ig = jax.nn.sigmoid(z[:, LRU_BLOCK:])
        log_a = r * neg_c_softplus[:, sl]
        a = jnp.exp(log_a)
        a_scr[:, sl] = a
        b_scr[:, sl] = jnp.sqrt(1.0 - a * a) * (ig * xh)

    row = lax.broadcasted_iota(jnp.int32, (SUBLANES, LRU_WIDTH), 0)

    def group(i, hprev):
        r0 = pl.multiple_of(i * SUBLANES, SUBLANES)
        a = a_scr[pl.ds(r0, SUBLANES), :]
        b = b_scr[pl.ds(r0, SUBLANES), :]
        for s in (1, 2, 4):
            keep = row >= s
            b = jnp.where(keep, a * pltpu.roll(b, s, 0) + b, b)
            a = jnp.where(keep, a * pltpu.roll(a, s, 0), a)
        hs = a * hprev + b
        b_scr[pl.ds(r0, SUBLANES), :] = hs
        return hs[SUBLANES - 1:SUBLANES, :]

    h_scr[...] = lax.fori_loop(0, TM // SUBLANES, group, h_scr[...], unroll=2)

    y = (_gelu_tanh(gate) * b_scr[...]).astype(BF16)
    m = jnp.dot(y, w_out[...], preferred_element_type=F32) + b_out[...]
    o_ref[0] = x + _rms(m, post_g[...])


def _lru_mixer(x, pre_g, post_g, w_in, b_in, conv_w, conv_b, w_ri, b_ri, lam, w_out, b_out):
    tok = pl.BlockSpec((1, TM, D_MODEL), lambda b, j: (b, j, 0))
    consts = (pre_g, post_g, w_in, b_in, conv_w, conv_b, w_ri, b_ri, lam, w_out, b_out)
    return pl.pallas_call(
        _lru_kernel,
        grid=(BATCH, SEQ // TM),
        in_specs=[tok] + [_const_spec(c.shape) for c in consts],
        out_specs=tok,
        out_shape=jax.ShapeDtypeStruct((BATCH, SEQ, D_MODEL), F32),
        scratch_shapes=[
            pltpu.VMEM((TM + SUBLANES, LRU_WIDTH), F32),
            pltpu.VMEM((TM, LRU_WIDTH), F32),
            pltpu.VMEM((TM, LRU_WIDTH), F32),
            pltpu.VMEM((1, LRU_WIDTH), F32),
        ],
        compiler_params=pltpu.CompilerParams(
            dimension_semantics=("arbitrary", "arbitrary"),
            vmem_limit_bytes=48 * 1024 * 1024,
        ),
        name="lru_mixer",
    )(x, *consts)


def _ffn_body(x, pre_g, post_g, w_up, cw, cb, w_dn, o_ref, h_scr):
    j = pl.program_id(1)

    @pl.when(j == 0)
    def _():
        h_scr[0:HALO, :] = jnp.zeros((HALO, D_MODEL), BF16)

    @pl.when(j > 0)
    def _():
        h_scr[0:HALO, :] = h_scr[TM:TM + HALO, :]

    h_scr[HALO:, :] = _rms(x, pre_g[...]).astype(BF16)
    hx = h_scr[...]

    acc = None
    for c in range(N_FFN_CHUNKS):
        u = jnp.dot(hx, w_up[c], preferred_element_type=F32)
        cu = u * cw[c, FFN_CONV - 1:FFN_CONV, :] + cb[c]
        for k in range(1, FFN_CONV):
            cu = cu + pltpu.roll(u, k, 0) * cw[c, FFN_CONV - 1 - k:FFN_CONV - k, :]
        cu = cu[HALO:, :]
        act = (_gelu_tanh(cu[:, :FFN_CHUNK]) * cu[:, FFN_CHUNK:]).astype(BF16)
        part = jnp.dot(act, w_dn[c], preferred_element_type=F32)
        acc = part if acc is None else acc + part
    o_ref[0] = x + _rms(acc, post_g[...])


def _ffn_kernel(x_ref, pre_g, post_g, w_up, cw, cb, w_dn, o_ref, h_scr):
    _ffn_body(x_ref[0], pre_g, post_g, w_up, cw, cb, w_dn, o_ref, h_scr)


def _proj_ffn_kernel(x_ref, a_ref, w_o, mix_g, pre_g, post_g, w_up, cw, cb, w_dn, o_ref, h_scr):
    m = jnp.dot(a_ref[0], w_o[...], preferred_element_type=F32)
    x = x_ref[0] + _rms(m, mix_g[...])
    _ffn_body(x, pre_g, post_g, w_up, cw, cb, w_dn, o_ref, h_scr)


def _ffn_params(w_up, conv_w, conv_b, w_down):
    def chunked(t):
        g = t[..., :D_FF].reshape(t.shape[:-1] + (N_FFN_CHUNKS, FFN_CHUNK))
        v = t[..., D_FF:].reshape(t.shape[:-1] + (N_FFN_CHUNKS, FFN_CHUNK))
        return jnp.moveaxis(jnp.concatenate([g, v], axis=-1), -2, 0)

    return (chunked(w_up).astype(BF16), chunked(conv_w), chunked(conv_b[None, :]),
            w_down.reshape(N_FFN_CHUNKS, FFN_CHUNK, D_MODEL).astype(BF16))


def _ffn_call(kernel, token_inputs, consts):
    tok = pl.BlockSpec((1, TM, D_MODEL), lambda b, j: (b, j, 0))
    return pl.pallas_call(
        kernel,
        grid=(BATCH, SEQ // TM),
        in_specs=[tok] * len(token_inputs) + [_const_spec(c.shape) for c in consts],
        out_specs=tok,
        out_shape=jax.ShapeDtypeStruct((BATCH, SEQ, D_MODEL), F32),
        scratch_shapes=[pltpu.VMEM((TM + HALO, D_MODEL), BF16)],
        compiler_params=pltpu.CompilerParams(
            dimension_semantics=("arbitrary", "arbitrary"),
            vmem_limit_bytes=56 * 1024 * 1024,
        ),
        name=kernel.__name__.strip("_"),
    )(*token_inputs, *consts)


def _mla_proj_kernel(x_ref, ct_ref, st_ref, pre_g, w_in, q_g, w_q, kv_g, w_kv, q_ref, k_ref, v_ref):
    h = _rms(x_ref[0], pre_g[...]).astype(BF16)
    c = jnp.dot(h, w_in[...], preferred_element_type=F32)
    qn = _rms(c[:, :Q_LORA], q_g[...]).astype(BF16)
    kvn = _rms(c[:, Q_LORA:Q_LORA + KV_LORA], kv_g[...]).astype(BF16)
    ct = ct_ref[0]
    st = st_ref[0]

    def rope(t):
        return t * ct + pltpu.roll(t, HEAD_LANES // 2, 1) * st

    kpe = rope(c[:, Q_LORA + KV_LORA:])
    q = jnp.dot(qn, w_q[...], preferred_element_type=F32)
    kv = jnp.dot(kvn, w_kv[...], preferred_element_type=F32)
    scale = QK_DIM ** -0.5
    for hd in range(MLA_HEADS):
        sl = slice(hd * HEAD_LANES, (hd + 1) * HEAD_LANES)
        q_ref[0, hd] = (rope(q[:, sl]) * scale).astype(BF16)
        k_ref[0, hd] = (kv[:, sl] + kpe).astype(BF16)
    v_ref[0] = kv[:, MLA_HEADS * HEAD_LANES:].astype(BF16)


def _head_layout(nope, x1, x2):
    lead = nope.shape[:-1]
    z = lambda n: jnp.zeros(lead + (n,), nope.dtype)
    return jnp.concatenate([
        z(ROPE_HALF) if x1 is None else x1, nope[..., :NOPE_A],
        z(ROPE_HALF) if x2 is None else x2, nope[..., NOPE_A:],
        z(HEAD_LANES - NOPE_B_OFF - (QK_NOPE - NOPE_A)),
    ], axis=-1)


def _mla_params(w_in, w_qb, w_kvb):
    lat = Q_LORA + KV_LORA
    zpad = jnp.zeros((D_MODEL, HEAD_LANES // 2 - ROPE_HALF), w_in.dtype)
    w_in_p = jnp.concatenate([w_in[:, :lat], w_in[:, lat:lat + ROPE_HALF], zpad,
                              w_in[:, lat + ROPE_HALF:], zpad], axis=-1)
    wq = w_qb.reshape(Q_LORA, MLA_HEADS, QK_DIM)
    wq_p = _head_layout(wq[..., :QK_NOPE], wq[..., QK_NOPE:QK_NOPE + ROPE_HALF],
                        wq[..., QK_NOPE + ROPE_HALF:]).reshape(Q_LORA, MLA_HEADS * HEAD_LANES)
    wkv = w_kvb.reshape(KV_LORA, MLA_HEADS, QK_NOPE + V_DIM)
    wk_p = _head_layout(wkv[..., :QK_NOPE], None, None).reshape(KV_LORA, MLA_HEADS * HEAD_LANES)
    wv = wkv[..., QK_NOPE:].reshape(KV_LORA, MLA_HEADS * V_DIM)
    return w_in_p.astype(BF16), wq_p.astype(BF16), jnp.concatenate([wk_p, wv], axis=-1).astype(BF16)


def _mla_proj(x, ctab, stab, pre_g, w_in_p, q_g, wq_p, kv_g, wkv_p):
    tok = pl.BlockSpec((1, TM, D_MODEL), lambda b, j: (b, j, 0))
    tab = pl.BlockSpec((1, TM, HEAD_LANES), lambda b, j: (b, j, 0))
    head = pl.BlockSpec((1, MLA_HEADS, TM, HEAD_LANES), lambda b, j: (b, 0, j, 0))
    consts = (pre_g, w_in_p, q_g, wq_p, kv_g, wkv_p)
    return pl.pallas_call(
        _mla_proj_kernel,
        grid=(BATCH, SEQ // TM),
        in_specs=[tok, tab, tab] + [_const_spec(c.shape) for c in consts],
        out_specs=(head, head, pl.BlockSpec((1, TM, MLA_HEADS * V_DIM), lambda b, j: (b, j, 0))),
        out_shape=(jax.ShapeDtypeStruct((BATCH, MLA_HEADS, SEQ, HEAD_LANES), BF16),
                   jax.ShapeDtypeStruct((BATCH, MLA_HEADS, SEQ, HEAD_LANES), BF16),
                   jax.ShapeDtypeStruct((BATCH, SEQ, MLA_HEADS * V_DIM), BF16)),
        compiler_params=pltpu.CompilerParams(
            dimension_semantics=("arbitrary", "arbitrary"),
            vmem_limit_bytes=48 * 1024 * 1024,
        ),
        name="mla_proj",
    )(x, ctab, stab, *consts)


def _attn_kernel(q_ref, k_ref, v_ref, o_ref):
    nt = (((1,), (1,)), ((), ()))

    def q_block(qi, _):
        q0 = pl.multiple_of(qi * TQ, TQ)
        qs = [q_ref[0, hh, pl.ds(q0, TQ), :] for hh in range(2)]

        def update(k0, state, masked):
            vblk = v_ref[0, pl.ds(k0, TQ), :]
            new = []
            for hh in range(2):
                m, l, acc = state[hh]
                s = lax.dot_general(qs[hh], k_ref[0, hh, pl.ds(k0, TQ), :], nt,
                                    preferred_element_type=F32)
                if masked:
                    qc = lax.broadcasted_iota(jnp.int32, (TQ, TQ), 0) // CHUNK
                    kc = lax.broadcasted_iota(jnp.int32, (TQ, TQ), 1) // CHUNK
                    s = jnp.where(kc <= qc, s, MASK_VALUE)
                m_new = jnp.maximum(m, jnp.max(s, axis=-1, keepdims=True))
                alpha = jnp.exp(m - m_new)
                p = jnp.exp(s - m_new)
                l = alpha * l + jnp.sum(p, axis=-1, keepdims=True)
                acc = alpha * acc + jnp.dot(p.astype(BF16), vblk, preferred_element_type=F32)
                new.append((m_new, l, acc))
            return tuple(new)

        init = tuple((jnp.full((TQ, 1), MASK_VALUE, F32), jnp.zeros((TQ, 1), F32),
                      jnp.zeros((TQ, 2 * V_DIM), F32)) for _ in range(2))
        state = lax.fori_loop(
            0, qi, lambda j, st: update(pl.multiple_of(j * TQ, TQ), st, False), init)
        (_, l0, acc0), (_, l1, acc1) = update(q0, state, True)
        first_head = lax.broadcasted_iota(jnp.int32, (TQ, 2 * V_DIM), 1) < V_DIM
        o_ref[0, pl.ds(q0, TQ), :] = jnp.where(first_head, acc0 / l0, acc1 / l1).astype(BF16)
        return 0

    lax.fori_loop(0, SEQ // TQ, q_block, 0)


def _attention(q, k, v):
    qk = pl.BlockSpec((1, 2, SEQ, HEAD_LANES), lambda b, p: (b, p, 0, 0))
    vo = pl.BlockSpec((1, SEQ, 2 * V_DIM), lambda b, p: (b, 0, p))
    return pl.pallas_call(
        _attn_kernel,
        grid=(BATCH, MLA_HEADS // 2),
        in_specs=[qk, qk, vo],
        out_specs=vo,
        out_shape=jax.ShapeDtypeStruct((BATCH, SEQ, MLA_HEADS * V_DIM), BF16),
        compiler_params=pltpu.CompilerParams(
            dimension_semantics=("arbitrary", "arbitrary"),
            vmem_limit_bytes=48 * 1024 * 1024,
        ),
        name="chunk_causal_attention",
    )(q, k, v)


def kernel(x, positions, mix_pre_g, mix_post_g, ffn_pre_g, ffn_post_g, lru_w_in, lru_b_in, lru_conv_w, lru_conv_b, lru_w_r, lru_b_r, lru_w_i, lru_b_i, lru_lambda, lru_w_out, lru_b_out, mla_w_in, mla_q_norm_g, mla_w_qb, mla_kv_norm_g, mla_w_kvb, mla_w_out, ffn_w_up, ffn_conv_w, ffn_conv_b, ffn_w_down):
    row = lambda t: t.reshape(1, -1)

    w_ri = jnp.concatenate([lru_w_r[0], lru_w_i[0]], axis=-1).astype(BF16)
    b_ri = jnp.concatenate([lru_b_r[0], lru_b_i[0]], axis=-1)[:, None, :]
    x = _lru_mixer(x, row(mix_pre_g[0]), row(mix_post_g[0]), lru_w_in[0].astype(BF16), row(lru_b_in[0]),
                   lru_conv_w[0], row(lru_conv_b[0]), w_ri, b_ri, row(lru_lambda[0]),
                   lru_w_out[0].astype(BF16), row(lru_b_out[0]))
    x = _ffn_call(_ffn_kernel, (x,),
                  (row(ffn_pre_g[0]), row(ffn_post_g[0]))
                  + _ffn_params(ffn_w_up[0], ffn_conv_w[0], ffn_conv_b[0], ffn_w_down[0]))

    ctab, stab = _rope_tables(positions)
    w_in_p, wq_p, wkv_p = _mla_params(mla_w_in[0], mla_w_qb[0], mla_w_kvb[0])
    q, k, v = _mla_proj(x, ctab, stab, row(mix_pre_g[1]), w_in_p, row(mla_q_norm_g[0]), wq_p,
                        row(mla_kv_norm_g[0]), wkv_p)
    a = _attention(q, k, v)
    x = _ffn_call(_proj_ffn_kernel, (x, a),
                  (mla_w_out[0].astype(BF16), row(mix_post_g[1]), row(ffn_pre_g[1]), row(ffn_post_g[1]))
                  + _ffn_params(ffn_w_up[1], ffn_conv_w[1], ffn_conv_b[1], ffn_w_down[1]))
    return x
```

```python
import functools
import math

import jax
import jax.numpy as jnp
from jax import lax
from jax.experimental import pallas as pl
from jax.experimental.pallas import tpu as pltpu

D_MODEL = 1024
BATCH = 4
SEQ = 4096
CHUNK = 64

LRU_WIDTH = 1024
LRU_HEADS = 4
LRU_BLOCK = LRU_WIDTH // LRU_HEADS
LRU_CONV = 4
LRU_C = 8.0

MLA_HEADS = 16
Q_LORA = 768
KV_LORA = 256
QK_NOPE = 64
QK_ROPE = 32
ROPE_HALF = QK_ROPE // 2
V_DIM = 64
QK_DIM = QK_NOPE + QK_ROPE
ROPE_THETA = 10000.0

D_FF = 2816
FFN_CONV = 3
EPS = 1e-6

LANES = 128
SUBLANES = 8
BF16_ROWS = 16

HEAD_LANES = LANES
NOPE_A = 48
X1_OFF = 0
NOPE_A_OFF = ROPE_HALF
X2_OFF = HEAD_LANES // 2
NOPE_B_OFF = X2_OFF + ROPE_HALF
PE_LANES = HEAD_LANES

TM = 512
FFN_CHUNK = 256
N_FFN_CHUNKS = D_FF // FFN_CHUNK
TQ = 512
HALO = BF16_ROWS

F32 = jnp.float32
BF16 = jnp.bfloat16
MASK_VALUE = -1e30


def _rms(x, g):
    ms = jnp.mean(x * x, axis=-1, keepdims=True)
    return x * lax.rsqrt(ms + EPS) * g


def _gelu_tanh(x):
    c = math.sqrt(2.0 / math.pi)
    return 0.5 * x * (1.0 + jnp.tanh(c * (x + 0.044715 * (x * x * x))))


def _const_spec(shape):
    nd = len(shape)
    return pl.BlockSpec(shape, lambda b, j: (0,) * nd, pipeline_mode=pl.Buffered(1))


def _rope_kernel(pos_ref, freq_ref, cos_ref, sin_ref):
    ang = pos_ref[...].astype(F32) * freq_ref[...]
    cos_ref[...] = jnp.cos(ang)
    sin_ref[...] = jnp.sin(ang)


def _rope_tables(positions):
    rows = BATCH * SEQ * ROPE_HALF // LANES
    pos_rep = jnp.repeat(positions.reshape(-1), ROPE_HALF).reshape(rows, LANES)
    inv_freq = 1.0 / (ROPE_THETA ** (jnp.arange(0, QK_ROPE, 2, dtype=F32) / QK_ROPE))
    freq_row = jnp.tile(inv_freq, LANES // ROPE_HALF).reshape(1, LANES)
    cos_d, sin_d = pl.pallas_call(
        _rope_kernel,
        out_shape=(jax.ShapeDtypeStruct((rows, LANES), F32),) * 2,
        name="rope_tables",
    )(pos_rep, freq_row)
    cos = cos_d.reshape(BATCH, SEQ, ROPE_HALF)
    sin = sin_d.reshape(BATCH, SEQ, ROPE_HALF)
    ones = jnp.ones((BATCH, SEQ, NOPE_A), F32)
    zeros = jnp.zeros((BATCH, SEQ, NOPE_A), F32)
    ctab = jnp.concatenate([cos, ones, cos, ones], axis=-1)
    stab = jnp.concatenate([-sin, zeros, sin, zeros], axis=-1)
    return ctab, stab


def _lru_kernel(x_ref, pre_g, post_g, w_in, b_in, conv_w, conv_b, w_ri, b_ri, lam, w_out, b_out,
                o_ref, xb_scr, a_scr, b_scr, h_scr):
    j = pl.program_id(1)
    x = x_ref[0]
    h = _rms(x, pre_g[...]).astype(BF16)
    u = jnp.dot(h, w_in[...], preferred_element_type=F32) + b_in[...]
    gate = u[:, :LRU_WIDTH]

    @pl.when(j == 0)
    def _():
        xb_scr[0:SUBLANES, :] = jnp.zeros((SUBLANES, LRU_WIDTH), F32)
        h_scr[...] = jnp.zeros((1, LRU_WIDTH), F32)

    @pl.when(j > 0)
    def _():
        xb_scr[0:SUBLANES, :] = xb_scr[TM:TM + SUBLANES, :]

    xb_scr[SUBLANES:, :] = u[:, LRU_WIDTH:]
    xe = xb_scr[...]
    xc = xe * conv_w[LRU_CONV - 1:LRU_CONV, :] + conv_b[...]
    for k in range(1, LRU_CONV):
        xc = xc + pltpu.roll(xe, k, 0) * conv_w[LRU_CONV - 1 - k:LRU_CONV - k, :]
    xc = xc[SUBLANES:, :]

    sp = lam[...]
    neg_c_softplus = -LRU_C * (jnp.maximum(-sp, 0.0) + jnp.log(1.0 + jnp.exp(-jnp.abs(sp))))
    for hd in range(LRU_HEADS):
        sl = slice(hd * LRU_BLOCK, (hd + 1) * LRU_BLOCK)
        xh = xc[:, sl]
        z = jnp.dot(xh.astype(BF16), w_ri[hd], preferred_element_type=F32) + b_ri[hd]
        r = jax.nn.sigmoid(z[:, :LRU_BLOCK])
        ig = jax.nn.sigmoid(z[:, LRU_BLOCK:])
        log_a = r * neg_c_softplus[:, sl]
        a = jnp.exp(log_a)
        a_scr[:, sl] = a
        b_scr[:, sl] = jnp.sqrt(1.0 - a * a) * (ig * xh)

    row = lax.broadcasted_iota(jnp.int32, (SUBLANES, LRU_WIDTH), 0)

    def group(i, hprev):
        r0 = pl.multiple_of(i * SUBLANES, SUBLANES)
        a = a_scr[pl.ds(r0, SUBLANES), :]
        b = b_scr[pl.ds(r0, SUBLANES), :]
        for s in (1, 2, 4):
            keep = row >= s
            b = jnp.where(keep, a * pltpu.roll(b, s, 0) + b, b)
            a = jnp.where(keep, a * pltpu.roll(a, s, 0), a)
        hs = a * hprev + b
        b_scr[pl.ds(r0, SUBLANES), :] = hs
        return hs[SUBLANES - 1:SUBLANES, :]

    h_scr[...] = lax.fori_loop(0, TM // SUBLANES, group, h_scr[...], unroll=2)

    y = (_gelu_tanh(gate) * b_scr[...]).astype(BF16)
    m = jnp.dot(y, w_out[...], preferred_element_type=F32) + b_out[...]
    o_ref[0] = x + _rms(m, post_g[...])


def _lru_mixer(x, pre_g, post_g, w_in, b_in, conv_w, conv_b, w_ri, b_ri, lam, w_out, b_out):
    tok = pl.BlockSpec((1, TM, D_MODEL), lambda b, j: (b, j, 0))
    consts = (pre_g, post_g, w_in, b_in, conv_w, conv_b, w_ri, b_ri, lam, w_out, b_out)
    return pl.pallas_call(
        _lru_kernel,
        grid=(BATCH, SEQ // TM),
        in_specs=[tok] + [_const_spec(c.shape) for c in consts],
        out_specs=tok,
        out_shape=jax.ShapeDtypeStruct((BATCH, SEQ, D_MODEL), F32),
        scratch_shapes=[
            pltpu.VMEM((TM + SUBLANES, LRU_WIDTH), F32),
            pltpu.VMEM((TM, LRU_WIDTH), F32),
            pltpu.VMEM((TM, LRU_WIDTH), F32),
            pltpu.VMEM((1, LRU_WIDTH), F32),
        ],
        compiler_params=pltpu.CompilerParams(
            dimension_semantics=("arbitrary", "arbitrary"),
            vmem_limit_bytes=48 * 1024 * 1024,
        ),
        name="lru_mixer",
    )(x, *consts)


def _ffn_body(x, pre_g, post_g, w_up, cw, cb, w_dn, o_ref, h_scr):
    j = pl.program_id(1)

    @pl.when(j == 0)
    def _():
        h_scr[0:HALO, :] = jnp.zeros((HALO, D_MODEL), BF16)

    @pl.when(j > 0)
    def _():
        h_scr[0:HALO, :] = h_scr[TM:TM + HALO, :]

    h_scr[HALO:, :] = _rms(x, pre_g[...]).astype(BF16)
    hx = h_scr[...]

    acc = None
    for c in range(N_FFN_CHUNKS):
        u = jnp.dot(hx, w_up[c], preferred_element_type=F32)
        cu = u * cw[c, FFN_CONV - 1:FFN_CONV, :] + cb[c]
        for k in range(1, FFN_CONV):
            cu = cu + pltpu.roll(u, k, 0) * cw[c, FFN_CONV - 1 - k:FFN_CONV - k, :]
        cu = cu[HALO:, :]
        act = (_gelu_tanh(cu[:, :FFN_CHUNK]) * cu[:, FFN_CHUNK:]).astype(BF16)
        part = jnp.dot(act, w_dn[c], preferred_element_type=F32)
        acc = part if acc is None else acc + part
    o_ref[0] = x + _rms(acc, post_g[...])


def _ffn_kernel(x_ref, pre_g, post_g, w_up, cw, cb, w_dn, o_ref, h_scr):
    _ffn_body(x_ref[0], pre_g, post_g, w_up, cw, cb, w_dn, o_ref, h_scr)


def _proj_ffn_kernel(x_ref, a_ref, w_o, mix_g, pre_g, post_g, w_up, cw, cb, w_dn, o_ref, h_scr):
    m = jnp.dot(a_ref[0], w_o[...], preferred_element_type=F32)
    x = x_ref[0] + _rms(m, mix_g[...])
    _ffn_body(x, pre_g, post_g, w_up, cw, cb, w_dn, o_ref, h_scr)


def _ffn_params(w_up, conv_w, conv_b, w_down):
    def chunked(t):
        g = t[..., :D_FF].reshape(t.shape[:-1] + (N_FFN_CHUNKS, FFN_CHUNK))
        v = t[..., D_FF:].reshape(t.shape[:-1] + (N_FFN_CHUNKS, FFN_CHUNK))
        return jnp.moveaxis(jnp.concatenate([g, v], axis=-1), -2, 0)

    return (chunked(w_up).astype(BF16), chunked(conv_w), chunked(conv_b[None, :]),
            w_down.reshape(N_FFN_CHUNKS, FFN_CHUNK, D_MODEL).astype(BF16))


def _ffn_call(kernel, token_inputs, consts):
    tok = pl.BlockSpec((1, TM, D_MODEL), lambda b, j: (b, j, 0))
    return pl.pallas_call(
        kernel,
        grid=(BATCH, SEQ // TM),
        in_specs=[tok] * len(token_inputs) + [_const_spec(c.shape) for c in consts],
        out_specs=tok,
        out_shape=jax.ShapeDtypeStruct((BATCH, SEQ, D_MODEL), F32),
        scratch_shapes=[pltpu.VMEM((TM + HALO, D_MODEL), BF16)],
        compiler_params=pltpu.CompilerParams(
            dimension_semantics=("arbitrary", "arbitrary"),
            vmem_limit_bytes=56 * 1024 * 1024,
        ),
        name=kernel.__name__.strip("_"),
    )(*token_inputs, *consts)


def _mla_proj_kernel(x_ref, ct_ref, st_ref, pre_g, w_in, q_g, w_q, kv_g, w_k, w_vt,
                     q_ref, k_ref, vt_ref):
    h = _rms(x_ref[0], pre_g[...]).astype(BF16)
    c = jnp.dot(h, w_in[...], preferred_element_type=F32)
    qn = _rms(c[:, :Q_LORA], q_g[...]).astype(BF16)
    kvn = _rms(c[:, Q_LORA:Q_LORA + KV_LORA], kv_g[...]).astype(BF16)
    ct = ct_ref[0]
    st = st_ref[0]

    def rope(t):
        return t * ct + pltpu.roll(t, HEAD_LANES // 2, 1) * st

    kpe = rope(c[:, Q_LORA + KV_LORA:])
    q = jnp.dot(qn, w_q[...], preferred_element_type=F32)
    kn = jnp.dot(kvn, w_k[...], preferred_element_type=F32)
    q_scale = QK_DIM ** -0.5 * math.log2(math.e)
    for hd in range(MLA_HEADS):
        sl = slice(hd * HEAD_LANES, (hd + 1) * HEAD_LANES)
        q_ref[0, hd] = (rope(q[:, sl]) * q_scale).astype(BF16)
        k_ref[0, hd] = (kn[:, sl] + kpe).astype(BF16)
    vt = lax.dot_general(w_vt[...], kvn, (((1,), (1,)), ((), ())), preferred_element_type=F32)
    vt_ref[0, :, 0] = vt.astype(BF16).reshape(MLA_HEADS // 2, 2 * V_DIM, TM)


def _head_layout(nope, x1, x2):
    lead = nope.shape[:-1]
    z = lambda n: jnp.zeros(lead + (n,), nope.dtype)
    return jnp.concatenate([
        z(ROPE_HALF) if x1 is None else x1, nope[..., :NOPE_A],
        z(ROPE_HALF) if x2 is None else x2, nope[..., NOPE_A:],
        z(HEAD_LANES - NOPE_B_OFF - (QK_NOPE - NOPE_A)),
    ], axis=-1)


def _mla_params(w_in, w_qb, w_kvb):
    lat = Q_LORA + KV_LORA
    zpad = jnp.zeros((D_MODEL, HEAD_LANES // 2 - ROPE_HALF), w_in.dtype)
    w_in_p = jnp.concatenate([w_in[:, :lat], w_in[:, lat:lat + ROPE_HALF], zpad,
                              w_in[:, lat + ROPE_HALF:], zpad], axis=-1)
    wq = w_qb.reshape(Q_LORA, MLA_HEADS, QK_DIM)
    wq_p = _head_layout(wq[..., :QK_NOPE], wq[..., QK_NOPE:QK_NOPE + ROPE_HALF],
                        wq[..., QK_NOPE + ROPE_HALF:]).reshape(Q_LORA, MLA_HEADS * HEAD_LANES)
    wkv = w_kvb.reshape(KV_LORA, MLA_HEADS, QK_NOPE + V_DIM)
    wk_p = _head_layout(wkv[..., :QK_NOPE], None, None).reshape(KV_LORA, MLA_HEADS * HEAD_LANES)
    wv_t = wkv[..., QK_NOPE:].reshape(KV_LORA, MLA_HEADS * V_DIM).T
    return w_in_p.astype(BF16), wq_p.astype(BF16), wk_p.astype(BF16), wv_t.astype(BF16)


def _mla_proj(x, ctab, stab, pre_g, w_in_p, q_g, wq_p, kv_g, wk_p, wv_t):
    tok = pl.BlockSpec((1, TM, D_MODEL), lambda b, j: (b, j, 0))
    tab = pl.BlockSpec((1, TM, HEAD_LANES), lambda b, j: (b, j, 0))
    head = pl.BlockSpec((1, MLA_HEADS, TM, HEAD_LANES), lambda b, j: (b, 0, j, 0))
    vt_blk = pl.BlockSpec((1, MLA_HEADS // 2, 1, 2 * V_DIM, TM), lambda b, j: (b, 0, j, 0, 0))
    consts = (pre_g, w_in_p, q_g, wq_p, kv_g, wk_p, wv_t)
    return pl.pallas_call(
        _mla_proj_kernel,
        grid=(BATCH, SEQ // TM),
        in_specs=[tok, tab, tab] + [_const_spec(c.shape) for c in consts],
        out_specs=(head, head, vt_blk),
        out_shape=(jax.ShapeDtypeStruct((BATCH, MLA_HEADS, SEQ, HEAD_LANES), BF16),
                   jax.ShapeDtypeStruct((BATCH, MLA_HEADS, SEQ, HEAD_LANES), BF16),
                   jax.ShapeDtypeStruct((BATCH, MLA_HEADS // 2, SEQ // TM, 2 * V_DIM, TM), BF16)),
        compiler_params=pltpu.CompilerParams(
            dimension_semantics=("arbitrary", "arbitrary"),
            vmem_limit_bytes=48 * 1024 * 1024,
        ),
        name="mla_proj",
    )(x, ctab, stab, *consts)


def _attn_kernel(q_ref, k_ref, vt_ref, o_ref, s_scr):
    nt = (((1,), (1,)), ((), ()))

    def scores(hh, q, j):
        k0 = pl.multiple_of(j * TQ, TQ)
        return lax.dot_general(k_ref[0, hh, pl.ds(k0, TQ), :], q, nt,
                               preferred_element_type=F32)

    def softmax_pv(s, vt, state, masked):
        m, l, acc = state
        if masked:
            kc = lax.broadcasted_iota(jnp.int32, (TQ, TQ), 0) // CHUNK
            qc = lax.broadcasted_iota(jnp.int32, (TQ, TQ), 1) // CHUNK
            s = jnp.where(kc <= qc, s, MASK_VALUE)
        m_new = jnp.maximum(m, jnp.max(s, axis=0, keepdims=True))
        alpha = jnp.exp2(m - m_new)
        p = jnp.exp2(s - m_new)
        l = alpha * l + jnp.sum(p, axis=0, keepdims=True)
        acc = alpha * acc + jnp.dot(vt, p.astype(BF16), preferred_element_type=F32)
        return m_new, l, acc

    def q_block(qi, _):
        q0 = pl.multiple_of(qi * TQ, TQ)
        q_a = q_ref[0, 0, pl.ds(q0, TQ), :]
        q_b = q_ref[0, 1, pl.ds(q0, TQ), :]
        s_scr[...] = scores(0, q_a, 0)

        def step(j, state, last):
            vt = vt_ref[0, 0, j]
            s_b = scores(1, q_b, j)
            st_a = softmax_pv(s_scr[...], vt, state[0], last)
            if not last:
                s_scr[...] = scores(0, q_a, j + 1)
            st_b = softmax_pv(s_b, vt, state[1], last)
            return st_a, st_b

        init = tuple((jnp.full((1, TQ), MASK_VALUE, F32), jnp.zeros((1, TQ), F32),
                      jnp.zeros((2 * V_DIM, TQ), F32)) for _ in range(2))
        state = lax.fori_loop(0, qi, lambda j, st: step(j, st, False), init)
        (_, l0, acc0), (_, l1, acc1) = step(qi, state, True)
        first_head = lax.broadcasted_iota(jnp.int32, (2 * V_DIM, TQ), 0) < V_DIM
        o_t = jnp.where(first_head, acc0 / l0, acc1 / l1)
        o_ref[0, pl.ds(q0, TQ), :] = o_t.T.astype(BF16)
        return 0

    lax.fori_loop(0, SEQ // TQ, q_block, 0)


def _attention(q, k, vt):
    qk = pl.BlockSpec((1, 2, SEQ, HEAD_LANES), lambda b, p: (b, p, 0, 0))
    vts = pl.BlockSpec((1, 1, SEQ // TQ, 2 * V_DIM, TQ), lambda b, p: (b, p, 0, 0, 0))
    vo = pl.BlockSpec((1, SEQ, 2 * V_DIM), lambda b, p: (b, 0, p))
    return pl.pallas_call(
        _attn_kernel,
        grid=(BATCH, MLA_HEADS // 2),
        in_specs=[qk, qk, vts],
        out_specs=vo,
        out_shape=jax.ShapeDtypeStruct((BATCH, SEQ, MLA_HEADS * V_DIM), BF16),
        scratch_shapes=[pltpu.VMEM((TQ, TQ), F32)],
        compiler_params=pltpu.CompilerParams(
            dimension_semantics=("arbitrary", "arbitrary"),
            vmem_limit_bytes=48 * 1024 * 1024,
        ),
        name="chunk_causal_attention",
    )(q, k, vt)


def kernel(x, positions, mix_pre_g, mix_post_g, ffn_pre_g, ffn_post_g, lru_w_in, lru_b_in, lru_conv_w, lru_conv_b, lru_w_r, lru_b_r, lru_w_i, lru_b_i, lru_lambda, lru_w_out, lru_b_out, mla_w_in, mla_q_norm_g, mla_w_qb, mla_kv_norm_g, mla_w_kvb, mla_w_out, ffn_w_up, ffn_conv_w, ffn_conv_b, ffn_w_down):
    row = lambda t: t.reshape(1, -1)

    w_ri = jnp.concatenate([lru_w_r[0], lru_w_i[0]], axis=-1).astype(BF16)
    b_ri = jnp.concatenate([lru_b_r[0], lru_b_i[0]], axis=-1)[:, None, :]
    x = _lru_mixer(x, row(mix_pre_g[0]), row(mix_post_g[0]), lru_w_in[0].astype(BF16), row(lru_b_in[0]),
                   lru_conv_w[0], row(lru_conv_b[0]), w_ri, b_ri, row(lru_lambda[0]),
                   lru_w_out[0].astype(BF16), row(lru_b_out[0]))
    x = _ffn_call(_ffn_kernel, (x,),
                  (row(ffn_pre_g[0]), row(ffn_post_g[0]))
                  + _ffn_params(ffn_w_up[0], ffn_conv_w[0], ffn_conv_b[0], ffn_w_down[0]))

    ctab, stab = _rope_tables(positions)
    w_in_p, wq_p, wk_p, wv_t = _mla_params(mla_w_in[0], mla_w_qb[0], mla_w_kvb[0])
    q, k, vt = _mla_proj(x, ctab, stab, row(mix_pre_g[1]), w_in_p, row(mla_q_norm_g[0]), wq_p,
                         row(mla_kv_norm_g[0]), wk_p, wv_t)
    a = _attention(q, k, vt)
    x = _ffn_call(_proj_ffn_kernel, (x, a),
                  (mla_w_out[0].astype(BF16), row(mix_post_g[1]), row(ffn_pre_g[1]), row(ffn_post_g[1]))
                  + _ffn_params(ffn_w_up[1], ffn_conv_w[1], ffn_conv_b[1], ffn_w_down[1]))
    return x
```

```python
import functools
import math

import jax
import jax.numpy as jnp
from jax import lax
from jax.experimental import pallas as pl
from jax.experimental.pallas import tpu as pltpu

D_MODEL = 1024
BATCH = 4
SEQ = 4096
CHUNK = 64

LRU_WIDTH = 1024
LRU_HEADS = 4
LRU_BLOCK = LRU_WIDTH // LRU_HEADS
LRU_CONV = 4
LRU_C = 8.0

MLA_HEADS = 16
Q_LORA = 768
KV_LORA = 256
QK_NOPE = 64
QK_ROPE = 32
ROPE_HALF = QK_ROPE // 2
V_DIM = 64
QK_DIM = QK_NOPE + QK_ROPE
ROPE_THETA = 10000.0

D_FF = 2816
FFN_CONV = 3
EPS = 1e-6

LANES = 128
SUBLANES = 8
BF16_ROWS = 16

HEAD_LANES = LANES
NOPE_A = 48
X1_OFF = 0
NOPE_A_OFF = ROPE_HALF
X2_OFF = HEAD_LANES // 2
NOPE_B_OFF = X2_OFF + ROPE_HALF
FEAT_OFF = NOPE_B_OFF + (QK_NOPE - NOPE_A)

TM = 512
FFN_CHUNK = 256
N_FFN_CHUNKS = D_FF // FFN_CHUNK
TQ = 512
N_ATTN_STEPS = (SEQ // TQ) * (SEQ // TQ + 1) // 2
HALO = BF16_ROWS
ONES_ROWS = BF16_ROWS
VT_ROWS = 2 * V_DIM + ONES_ROWS

F32 = jnp.float32
BF16 = jnp.bfloat16
MASK_VALUE = -1e30
MASK_FEATURE = -2.0 ** 100


def _rms(x, g):
    ms = jnp.mean(x * x, axis=-1, keepdims=True)
    return x * lax.rsqrt(ms + EPS) * g


def _gelu_tanh(x):
    c = math.sqrt(2.0 / math.pi)
    return 0.5 * x * (1.0 + jnp.tanh(c * (x + 0.044715 * (x * x * x))))


def _const_spec(shape):
    nd = len(shape)
    return pl.BlockSpec(shape, lambda b, j: (0,) * nd, pipeline_mode=pl.Buffered(1))


def _rope_kernel(pos_ref, freq_ref, cos_ref, sin_ref):
    ang = pos_ref[...].astype(F32) * freq_ref[...]
    cos_ref[...] = jnp.cos(ang)
    sin_ref[...] = jnp.sin(ang)


def _rope_tables(positions):
    rows = BATCH * SEQ * ROPE_HALF // LANES
    pos_rep = jnp.repeat(positions.reshape(-1), ROPE_HALF).reshape(rows, LANES)
    inv_freq = 1.0 / (ROPE_THETA ** (jnp.arange(0, QK_ROPE, 2, dtype=F32) / QK_ROPE))
    freq_row = jnp.tile(inv_freq, LANES // ROPE_HALF).reshape(1, LANES)
    cos_d, sin_d = pl.pallas_call(
        _rope_kernel,
        out_shape=(jax.ShapeDtypeStruct((rows, LANES), F32),) * 2,
        name="rope_tables",
    )(pos_rep, freq_row)
    cos = cos_d.reshape(BATCH, SEQ, ROPE_HALF)
    sin = sin_d.reshape(BATCH, SEQ, ROPE_HALF)
    ones = jnp.ones((BATCH, SEQ, NOPE_A), F32)
    zeros = jnp.zeros((BATCH, SEQ, NOPE_A), F32)
    ctab = jnp.concatenate([cos, ones, cos, ones], axis=-1)
    stab = jnp.concatenate([-sin, zeros, sin, zeros], axis=-1)
    return ctab, stab


def _lru_kernel(x_ref, pre_g, post_g, w_in, b_in, conv_w, conv_b, w_ri, b_ri, lam, w_out, b_out,
                o_ref, xb_scr, a_scr, b_scr, h_scr):
    j = pl.program_id(1)
    x = x_ref[0]
    h = _rms(x, pre_g[...]).astype(BF16)
    u = jnp.dot(h, w_in[...], preferred_element_type=F32) + b_in[...]
    gate = u[:, :LRU_WIDTH]

    @pl.when(j == 0)
    def _():
        xb_scr[0:SUBLANES, :] = jnp.zeros((SUBLANES, LRU_WIDTH), F32)
        h_scr[...] = jnp.zeros((1, LRU_WIDTH), F32)

    @pl.when(j > 0)
    def _():
        xb_scr[0:SUBLANES, :] = xb_scr[TM:TM + SUBLANES, :]

    xb_scr[SUBLANES:, :] = u[:, LRU_WIDTH:]
    xe = xb_scr[...]
    xc = xe * conv_w[LRU_CONV - 1:LRU_CONV, :] + conv_b[...]
    for k in range(1, LRU_CONV):
        xc = xc + pltpu.roll(xe, k, 0) * conv_w[LRU_CONV - 1 - k:LRU_CONV - k, :]
    xc = xc[SUBLANES:, :]

    sp = lam[...]
    neg_c_softplus = -LRU_C * (jnp.maximum(-sp, 0.0) + jnp.log(1.0 + jnp.exp(-jnp.abs(sp))))
    for hd in range(LRU_HEADS):
        sl = slice(hd * LRU_BLOCK, (hd + 1) * LRU_BLOCK)
        xh = xc[:, sl]
        z = jnp.dot(xh.astype(BF16), w_ri[hd], preferred_element_type=F32) + b_ri[hd]
        r = jax.nn.sigmoid(z[:, :LRU_BLOCK])
        ig = jax.nn.sigmoid(z[:, LRU_BLOCK:])
        log_a = r * neg_c_softplus[:, sl]
        a = jnp.exp(log_a)
        a_scr[:, sl] = a
        b_scr[:, sl] = jnp.sqrt(1.0 - a * a) * (ig * xh)

    row = lax.broadcasted_iota(jnp.int32, (SUBLANES, LRU_WIDTH), 0)

    def group(i, hprev):
        r0 = pl.multiple_of(i * SUBLANES, SUBLANES)
        a = a_scr[pl.ds(r0, SUBLANES), :]
        b = b_scr[pl.ds(r0, SUBLANES), :]
        for s in (1, 2, 4):
            keep = row >= s
            b = jnp.where(keep, a * pltpu.roll(b, s, 0) + b, b)
            a = jnp.where(keep, a * pltpu.roll(a, s, 0), a)
        hs = a * hprev + b
        b_scr[pl.ds(r0, SUBLANES), :] = hs
        return hs[SUBLANES - 1:SUBLANES, :]

    h_scr[...] = lax.fori_loop(0, TM // SUBLANES, group, h_scr[...], unroll=2)

    y = (_gelu_tanh(gate) * b_scr[...]).astype(BF16)
    m = jnp.dot(y, w_out[...], preferred_element_type=F32) + b_out[...]
    o_ref[0] = x + _rms(m, post_g[...])


def _lru_mixer(x, pre_g, post_g, w_in, b_in, conv_w, conv_b, w_ri, b_ri, lam, w_out, b_out):
    tok = pl.BlockSpec((1, TM, D_MODEL), lambda b, j: (b, j, 0))
    consts = (pre_g, post_g, w_in, b_in, conv_w, conv_b, w_ri, b_ri, lam, w_out, b_out)
    return pl.pallas_call(
        _lru_kernel,
        grid=(BATCH, SEQ // TM),
        in_specs=[tok] + [_const_spec(c.shape) for c in consts],
        out_specs=tok,
        out_shape=jax.ShapeDtypeStruct((BATCH, SEQ, D_MODEL), F32),
        scratch_shapes=[
            pltpu.VMEM((TM + SUBLANES, LRU_WIDTH), F32),
            pltpu.VMEM((TM, LRU_WIDTH), F32),
            pltpu.VMEM((TM, LRU_WIDTH), F32),
            pltpu.VMEM((1, LRU_WIDTH), F32),
        ],
        compiler_params=pltpu.CompilerParams(
            dimension_semantics=("arbitrary", "arbitrary"),
            vmem_limit_bytes=48 * 1024 * 1024,
        ),
        name="lru_mixer",
    )(x, *consts)


def _ffn_body(zero_ref, x, pre_g, post_g, w_up, cw, cb, w_dn, o_ref, u_scrs, tail_scr):
    j = pl.program_id(1)
    z = zero_ref[0]

    @pl.when(j == 0)
    def _():
        tail_scr[...] = jnp.zeros(tail_scr.shape, F32)

    h = _rms(x, pre_g[...]).astype(BF16)

    def up(c):
        u_scrs[c % 2][z, SUBLANES:, :] = jnp.dot(h, w_up[c], preferred_element_type=F32)

    acc = None
    up(0)
    for c in range(N_FFN_CHUNKS):
        u_scr = u_scrs[c % 2]
        if c + 1 < N_FFN_CHUNKS:
            up(c + 1)
        u_scr[z, 0:SUBLANES, :] = tail_scr[c]
        tail_scr[c] = u_scr[z, TM:TM + SUBLANES, :]
        u = u_scr[z]
        cu = u * cw[c, FFN_CONV - 1:FFN_CONV, :] + cb[c]
        for k in range(1, FFN_CONV):
            cu = cu + pltpu.roll(u, k, 0) * cw[c, FFN_CONV - 1 - k:FFN_CONV - k, :]
        cu = cu[SUBLANES:, :]
        act = (_gelu_tanh(cu[:, :FFN_CHUNK]) * cu[:, FFN_CHUNK:]).astype(BF16)
        part = jnp.dot(act, w_dn[c], preferred_element_type=F32)
        acc = part if acc is None else acc + part
    o_ref[0] = x + _rms(acc, post_g[...])


def _ffn_kernel(zero_ref, x_ref, pre_g, post_g, w_up, cw, cb, w_dn, o_ref, u_scr0, u_scr1, tail_scr):
    _ffn_body(zero_ref, x_ref[0], pre_g, post_g, w_up, cw, cb, w_dn, o_ref, (u_scr0, u_scr1),
              tail_scr)


def _proj_ffn_kernel(zero_ref, x_ref, a_ref, w_o, mix_g, pre_g, post_g, w_up, cw, cb, w_dn, o_ref,
                     u_scr0, u_scr1, tail_scr):
    m = jnp.dot(a_ref[0], w_o[...], preferred_element_type=F32)
    x = x_ref[0] + _rms(m, mix_g[...])
    _ffn_body(zero_ref, x, pre_g, post_g, w_up, cw, cb, w_dn, o_ref, (u_scr0, u_scr1), tail_scr)


def _ffn_params(w_up, conv_w, conv_b, w_down):
    def chunked(t):
        g = t[..., :D_FF].reshape(t.shape[:-1] + (N_FFN_CHUNKS, FFN_CHUNK))
        v = t[..., D_FF:].reshape(t.shape[:-1] + (N_FFN_CHUNKS, FFN_CHUNK))
        return jnp.moveaxis(jnp.concatenate([g, v], axis=-1), -2, 0)

    return (chunked(w_up).astype(BF16), chunked(conv_w), chunked(conv_b[None, :]),
            w_down.reshape(N_FFN_CHUNKS, FFN_CHUNK, D_MODEL).astype(BF16))


def _ffn_call(kernel, token_inputs, consts):
    tok = pl.BlockSpec((1, TM, D_MODEL), lambda b, j: (b, j, 0))
    return pl.pallas_call(
        kernel,
        grid=(BATCH, SEQ // TM),
        in_specs=[pl.BlockSpec(memory_space=pltpu.SMEM)] + [tok] * len(token_inputs)
        + [_const_spec(c.shape) for c in consts],
        out_specs=tok,
        out_shape=jax.ShapeDtypeStruct((BATCH, SEQ, D_MODEL), F32),
        scratch_shapes=[pltpu.VMEM((1, SUBLANES + TM, 2 * FFN_CHUNK), F32),
                        pltpu.VMEM((1, SUBLANES + TM, 2 * FFN_CHUNK), F32),
                        pltpu.VMEM((N_FFN_CHUNKS, SUBLANES, 2 * FFN_CHUNK), F32)],
        compiler_params=pltpu.CompilerParams(
            dimension_semantics=("arbitrary", "arbitrary"),
            vmem_limit_bytes=56 * 1024 * 1024,
        ),
        name=kernel.__name__.strip("_"),
    )(jnp.zeros((1,), jnp.int32), *token_inputs, *consts)


def _mla_proj_kernel(x_ref, ct_ref, st_ref, pre_g, w_in, q_g, w_q, kv_g, w_k, w_vt, q_feat,
                     q_ref, k_ref, vt_ref):
    h = _rms(x_ref[0], pre_g[...]).astype(BF16)
    c = jnp.dot(h, w_in[...], preferred_element_type=F32)
    qn = _rms(c[:, :Q_LORA], q_g[...]).astype(BF16)
    kvn = _rms(c[:, Q_LORA:Q_LORA + KV_LORA], kv_g[...]).astype(BF16)
    ct = ct_ref[0]
    st = st_ref[0]

    def rope(t):
        return t * ct + pltpu.roll(t, HEAD_LANES // 2, 1) * st

    kpe = rope(c[:, Q_LORA + KV_LORA:])
    q = jnp.dot(qn, w_q[...], preferred_element_type=F32)
    kn = jnp.dot(kvn, w_k[...], preferred_element_type=F32)
    q_scale = QK_DIM ** -0.5 * math.log2(math.e)
    for hd in range(MLA_HEADS):
        sl = slice(hd * HEAD_LANES, (hd + 1) * HEAD_LANES)
        q_ref[0, hd] = (rope(q[:, sl]) * q_scale + q_feat[...]).astype(BF16)
        k_ref[0, hd] = (kn[:, sl] + kpe).astype(BF16)
    vt = lax.dot_general(w_vt[...], kvn, (((1,), (1,)), ((), ())), preferred_element_type=F32)
    vt_ref[0, :, 0, 0:2 * V_DIM, :] = vt.astype(BF16).reshape(MLA_HEADS // 2, 2 * V_DIM, TM)
    vt_ref[0, :, 0, 2 * V_DIM:, :] = jnp.ones((MLA_HEADS // 2, ONES_ROWS, TM), BF16)


def _head_layout(nope, x1, x2):
    lead = nope.shape[:-1]
    z = lambda n: jnp.zeros(lead + (n,), nope.dtype)
    return jnp.concatenate([
        z(ROPE_HALF) if x1 is None else x1, nope[..., :NOPE_A],
        z(ROPE_HALF) if x2 is None else x2, nope[..., NOPE_A:],
        z(HEAD_LANES - NOPE_B_OFF - (QK_NOPE - NOPE_A)),
    ], axis=-1)


def _mla_params(w_in, w_qb, w_kvb):
    lat = Q_LORA + KV_LORA
    zpad = jnp.zeros((D_MODEL, HEAD_LANES // 2 - ROPE_HALF), w_in.dtype)
    w_in_p = jnp.concatenate([w_in[:, :lat], w_in[:, lat:lat + ROPE_HALF], zpad,
                              w_in[:, lat + ROPE_HALF:], zpad], axis=-1)
    wq = w_qb.reshape(Q_LORA, MLA_HEADS, QK_DIM)
    wq_p = _head_layout(wq[..., :QK_NOPE], wq[..., QK_NOPE:QK_NOPE + ROPE_HALF],
                        wq[..., QK_NOPE + ROPE_HALF:]).reshape(Q_LORA, MLA_HEADS * HEAD_LANES)
    wkv = w_kvb.reshape(KV_LORA, MLA_HEADS, QK_NOPE + V_DIM)
    wk_p = _head_layout(wkv[..., :QK_NOPE], None, None).reshape(KV_LORA, MLA_HEADS * HEAD_LANES)
    wv_t = wkv[..., QK_NOPE:].reshape(KV_LORA, MLA_HEADS * V_DIM).T
    return w_in_p.astype(BF16), wq_p.astype(BF16), wk_p.astype(BF16), wv_t.astype(BF16)


def _mla_proj(x, ctab, stab, pre_g, w_in_p, q_g, wq_p, kv_g, wk_p, wv_t):
    tok = pl.BlockSpec((1, TM, D_MODEL), lambda b, j: (b, j, 0))
    tab = pl.BlockSpec((1, TM, HEAD_LANES), lambda b, j: (b, j, 0))
    head = pl.BlockSpec((1, MLA_HEADS, TM, HEAD_LANES), lambda b, j: (b, 0, j, 0))
    vt_blk = pl.BlockSpec((1, MLA_HEADS // 2, 1, VT_ROWS, TM), lambda b, j: (b, 0, j, 0, 0))
    consts = (pre_g, w_in_p, q_g, wq_p, kv_g, wk_p, wv_t, _mask_features()[0])
    return pl.pallas_call(
        _mla_proj_kernel,
        grid=(BATCH, SEQ // TM),
        in_specs=[tok, tab, tab] + [_const_spec(c.shape) for c in consts],
        out_specs=(head, head, vt_blk),
        out_shape=(jax.ShapeDtypeStruct((BATCH, MLA_HEADS, SEQ, HEAD_LANES), BF16),
                   jax.ShapeDtypeStruct((BATCH, MLA_HEADS, SEQ, HEAD_LANES), BF16),
                   jax.ShapeDtypeStruct((BATCH, MLA_HEADS // 2, SEQ // TM, VT_ROWS, TM), BF16)),
        compiler_params=pltpu.CompilerParams(
            dimension_semantics=("arbitrary", "arbitrary"),
            vmem_limit_bytes=48 * 1024 * 1024,
        ),
        name="mla_proj",
    )(x, ctab, stab, *consts)


def _mask_features():
    n_chunks = TQ // CHUNK
    chunk = lax.broadcasted_iota(jnp.int32, (TQ, HEAD_LANES), 0) // CHUNK
    c = lax.broadcasted_iota(jnp.int32, (TQ, HEAD_LANES), 1) - FEAT_OFF
    lane_ok = (c >= 0) & (c < n_chunks)
    q_feat = jnp.where(lane_ok & (chunk < c), 1.0, 0.0).astype(F32)
    k_feat = jnp.where(lane_ok & (chunk == c), MASK_FEATURE, 0.0).astype(BF16)
    return q_feat, jnp.stack([jnp.zeros_like(k_feat), k_feat])


def _attn_kernel(tab_ref, q_ref, k_ref, vt_ref, kf_ref, o_ref, s_scr, acc_scr):
    nt = (((1,), (1,)), ((), ()))

    def scores(hh, t):
        q0 = pl.multiple_of(tab_ref[0, t] * TQ, TQ)
        k0 = pl.multiple_of(tab_ref[1, t] * TQ, TQ)
        k = k_ref[0, hh, pl.ds(k0, TQ), :] + kf_ref[tab_ref[2, t]]
        return lax.dot_general(k, q_ref[0, hh, pl.ds(q0, TQ), :], nt,
                               preferred_element_type=F32)

    def softmax_pv(s, vt, state, first):
        m, acc = state
        m = jnp.where(first, MASK_VALUE, m)
        m_new = jnp.maximum(m, jnp.max(s, axis=0, keepdims=True))
        alpha = jnp.exp2(m - m_new)
        p = jnp.exp2(s - m_new)
        acc = alpha * acc + jnp.dot(vt, p.astype(BF16), preferred_element_type=F32)
        return m_new, acc

    s_scr[0] = scores(0, 0)

    def step(t, state, par):
        qi = tab_ref[0, t]
        first = tab_ref[3, t] == 1
        vt = vt_ref[0, 0, tab_ref[1, t]]
        s_scr[2 + par] = scores(1, t)
        st_a = softmax_pv(s_scr[par], vt, state[0], first)
        s_scr[1 - par] = scores(0, t + 1)
        st_b = softmax_pv(s_scr[2 + par], vt, state[1], first)
        acc_scr[qi, 0] = st_a[1]
        acc_scr[qi, 1] = st_b[1]
        return st_a, st_b

    def two_steps(i, state):
        return step(2 * i + 1, step(2 * i, state, 0), 1)

    init = tuple((jnp.full((1, TQ), MASK_VALUE, F32), jnp.zeros((VT_ROWS, TQ), F32))
                 for _ in range(2))
    lax.fori_loop(0, N_ATTN_STEPS // 2, two_steps, init)

    first_head = lax.broadcasted_iota(jnp.int32, (2 * V_DIM, TQ), 0) < V_DIM
    for qi in range(SEQ // TQ):
        num = [acc_scr[qi, hh, 0:2 * V_DIM, :] for hh in range(2)]
        den = [acc_scr[qi, hh, 2 * V_DIM:2 * V_DIM + 1, :] for hh in range(2)]
        o_t = jnp.where(first_head, num[0] / den[0], num[1] / den[1])
        o_ref[0, qi * TQ:(qi + 1) * TQ, :] = o_t.T.astype(BF16)


def _attn_steps():
    steps = [(qi, j, int(j == qi), int(j == 0)) for qi in range(SEQ // TQ) for j in range(qi + 1)]
    steps.append(steps[-1])
    return jnp.asarray(list(zip(*steps)), dtype=jnp.int32)


def _attention(q, k, vt):
    tab, k_feat = _attn_steps(), _mask_features()[1]
    qk = pl.BlockSpec((1, 2, SEQ, HEAD_LANES), lambda b, p: (b, p, 0, 0))
    vts = pl.BlockSpec((1, 1, SEQ // TQ, VT_ROWS, TQ), lambda b, p: (b, p, 0, 0, 0))
    vo = pl.BlockSpec((1, SEQ, 2 * V_DIM), lambda b, p: (b, 0, p))
    return pl.pallas_call(
        _attn_kernel,
        grid=(BATCH, MLA_HEADS // 2),
        in_specs=[pl.BlockSpec(memory_space=pltpu.SMEM), qk, qk, vts, _const_spec(k_feat.shape)],
        out_specs=vo,
        out_shape=jax.ShapeDtypeStruct((BATCH, SEQ, MLA_HEADS * V_DIM), BF16),
        scratch_shapes=[pltpu.VMEM((4, TQ, TQ), F32),
                        pltpu.VMEM((SEQ // TQ, 2, VT_ROWS, TQ), F32)],
        compiler_params=pltpu.CompilerParams(
            dimension_semantics=("arbitrary", "arbitrary"),
            vmem_limit_bytes=48 * 1024 * 1024,
        ),
        name="chunk_causal_attention",
    )(tab, q, k, vt, k_feat)


def kernel(x, positions, mix_pre_g, mix_post_g, ffn_pre_g, ffn_post_g, lru_w_in, lru_b_in, lru_conv_w, lru_conv_b, lru_w_r, lru_b_r, lru_w_i, lru_b_i, lru_lambda, lru_w_out, lru_b_out, mla_w_in, mla_q_norm_g, mla_w_qb, mla_kv_norm_g, mla_w_kvb, mla_w_out, ffn_w_up, ffn_conv_w, ffn_conv_b, ffn_w_down):
    row = lambda t: t.reshape(1, -1)

    w_ri = jnp.concatenate([lru_w_r[0], lru_w_i[0]], axis=-1).astype(BF16)
    b_ri = jnp.concatenate([lru_b_r[0], lru_b_i[0]], axis=-1)[:, None, :]
    x = _lru_mixer(x, row(mix_pre_g[0]), row(mix_post_g[0]), lru_w_in[0].astype(BF16), row(lru_b_in[0]),
                   lru_conv_w[0], row(lru_conv_b[0]), w_ri, b_ri, row(lru_lambda[0]),
                   lru_w_out[0].astype(BF16), row(lru_b_out[0]))
    x = _ffn_call(_ffn_kernel, (x,),
                  (row(ffn_pre_g[0]), row(ffn_post_g[0]))
                  + _ffn_params(ffn_w_up[0], ffn_conv_w[0], ffn_conv_b[0], ffn_w_down[0]))

    ctab, stab = _rope_tables(positions)
    w_in_p, wq_p, wk_p, wv_t = _mla_params(mla_w_in[0], mla_w_qb[0], mla_w_kvb[0])
    q, k, vt = _mla_proj(x, ctab, stab, row(mix_pre_g[1]), w_in_p, row(mla_q_norm_g[0]), wq_p,
                         row(mla_kv_norm_g[0]), wk_p, wv_t)
    a = _attention(q, k, vt)
    x = _ffn_call(_proj_ffn_kernel, (x, a),
                  (mla_w_out[0].astype(BF16), row(mix_post_g[1]), row(ffn_pre_g[1]), row(ffn_post_g[1]))
                  + _ffn_params(ffn_w_up[1], ffn_conv_w[1], ffn_conv_b[1], ffn_w_down[1]))
    return x
```

```python
import functools
import math

import jax
import jax.numpy as jnp
from jax import lax
from jax.experimental import pallas as pl
from jax.experimental.pallas import tpu as pltpu

D_MODEL = 1024
BATCH = 4
SEQ = 4096
CHUNK = 64

LRU_WIDTH = 1024
LRU_HEADS = 4
LRU_BLOCK = LRU_WIDTH // LRU_HEADS
LRU_CONV = 4
LRU_C = 8.0

MLA_HEADS = 16
Q_LORA = 768
KV_LORA = 256
QK_NOPE = 64
QK_ROPE = 32
ROPE_HALF = QK_ROPE // 2
V_DIM = 64
QK_DIM = QK_NOPE + QK_ROPE
ROPE_THETA = 10000.0

D_FF = 2816
FFN_CONV = 3
EPS = 1e-6

LANES = 128
SUBLANES = 8
BF16_ROWS = 16

HEAD_LANES = LANES
NOPE_A = 48
X1_OFF = 0
NOPE_A_OFF = ROPE_HALF
X2_OFF = HEAD_LANES // 2
NOPE_B_OFF = X2_OFF + ROPE_HALF
FEAT_OFF = NOPE_B_OFF + (QK_NOPE - NOPE_A)

TM = 512
FFN_CHUNK = 256
N_FFN_CHUNKS = D_FF // FFN_CHUNK
TQ = 512
N_ATTN_STEPS = (SEQ // TQ) * (SEQ // TQ + 1) // 2
ATTN_UNROLL = 4
HALO = BF16_ROWS
ONES_ROWS = BF16_ROWS
VT_ROWS = 2 * V_DIM + ONES_ROWS

F32 = jnp.float32
BF16 = jnp.bfloat16
MASK_VALUE = -1e30
MASK_FEATURE = -2.0 ** 100


def _rms(x, g):
    ms = jnp.mean(x * x, axis=-1, keepdims=True)
    return x * lax.rsqrt(ms + EPS) * g


def _gelu_tanh(x):
    c = math.sqrt(2.0 / math.pi)
    return 0.5 * x * (1.0 + jnp.tanh(c * (x + 0.044715 * (x * x * x))))


def _twice_gelu_times(x, v):
    c0 = math.sqrt(2.0 / math.pi)
    t = jnp.tanh(x * (c0 + (c0 * 0.044715) * (x * x)))
    return (x * v) * (1.0 + t)


def _const_spec(shape):
    nd = len(shape)
    return pl.BlockSpec(shape, lambda b, j: (0,) * nd, pipeline_mode=pl.Buffered(1))


def _rope_kernel(pos_ref, freq_ref, cos_ref, sin_ref):
    ang = pos_ref[...].astype(F32) * freq_ref[...]
    cos_ref[...] = jnp.cos(ang)
    sin_ref[...] = jnp.sin(ang)


def _rope_tables(positions):
    rows = BATCH * SEQ * ROPE_HALF // LANES
    pos_rep = jnp.repeat(positions.reshape(-1), ROPE_HALF).reshape(rows, LANES)
    inv_freq = 1.0 / (ROPE_THETA ** (jnp.arange(0, QK_ROPE, 2, dtype=F32) / QK_ROPE))
    freq_row = jnp.tile(inv_freq, LANES // ROPE_HALF).reshape(1, LANES)
    cos_d, sin_d = pl.pallas_call(
        _rope_kernel,
        out_shape=(jax.ShapeDtypeStruct((rows, LANES), F32),) * 2,
        name="rope_tables",
    )(pos_rep, freq_row)
    cos = cos_d.reshape(BATCH, SEQ, ROPE_HALF)
    sin = sin_d.reshape(BATCH, SEQ, ROPE_HALF)
    ones = jnp.ones((BATCH, SEQ, NOPE_A), F32)
    zeros = jnp.zeros((BATCH, SEQ, NOPE_A), F32)
    ctab = jnp.concatenate([cos, ones, cos, ones], axis=-1)
    stab = jnp.concatenate([-sin, zeros, sin, zeros], axis=-1)
    return ctab, stab


def _lru_kernel(x_ref, pre_g, post_g, w_in, b_in, conv_w, conv_b, w_ri, b_ri, lam, w_out, b_out,
                o_ref, xb_scr, a_scr, b_scr, h_scr):
    j = pl.program_id(1)
    x = x_ref[0]
    h = _rms(x, pre_g[...]).astype(BF16)
    u = jnp.dot(h, w_in[...], preferred_element_type=F32) + b_in[...]
    gate = u[:, :LRU_WIDTH]

    @pl.when(j == 0)
    def _():
        xb_scr[0:SUBLANES, :] = jnp.zeros((SUBLANES, LRU_WIDTH), F32)
        h_scr[...] = jnp.zeros((1, LRU_WIDTH), F32)

    @pl.when(j > 0)
    def _():
        xb_scr[0:SUBLANES, :] = xb_scr[TM:TM + SUBLANES, :]

    xb_scr[SUBLANES:, :] = u[:, LRU_WIDTH:]
    xe = xb_scr[...]
    xc = xe * conv_w[LRU_CONV - 1:LRU_CONV, :] + conv_b[...]
    for k in range(1, LRU_CONV):
        xc = xc + pltpu.roll(xe, k, 0) * conv_w[LRU_CONV - 1 - k:LRU_CONV - k, :]
    xc = xc[SUBLANES:, :]

    sp = lam[...]
    neg_c_softplus = -LRU_C * (jnp.maximum(-sp, 0.0) + jnp.log(1.0 + jnp.exp(-jnp.abs(sp))))
    for hd in range(LRU_HEADS):
        sl = slice(hd * LRU_BLOCK, (hd + 1) * LRU_BLOCK)
        xh = xc[:, sl]
        z = jnp.dot(xh.astype(BF16), w_ri[hd], preferred_element_type=F32) + b_ri[hd]
        r = jax.nn.sigmoid(z[:, :LRU_BLOCK])
        ig = jax.nn.sigmoid(z[:, LRU_BLOCK:])
        log_a = r * neg_c_softplus[:, sl]
        a = jnp.exp(log_a)
        a_scr[:, sl] = a
        b_scr[:, sl] = jnp.sqrt(1.0 - a * a) * (ig * xh)

    row = lax.broadcasted_iota(jnp.int32, (SUBLANES, LRU_WIDTH), 0)

    def group(i, hprev):
        r0 = pl.multiple_of(i * SUBLANES, SUBLANES)
        a = a_scr[pl.ds(r0, SUBLANES), :]
        b = b_scr[pl.ds(r0, SUBLANES), :]
        for s in (1, 2, 4):
            keep = row >= s
            b = jnp.where(keep, a * pltpu.roll(b, s, 0) + b, b)
            a = jnp.where(keep, a * pltpu.roll(a, s, 0), a)
        hs = a * hprev + b
        b_scr[pl.ds(r0, SUBLANES), :] = hs
        return hs[SUBLANES - 1:SUBLANES, :]

    h_scr[...] = lax.fori_loop(0, TM // SUBLANES, group, h_scr[...], unroll=2)

    y = (_gelu_tanh(gate) * b_scr[...]).astype(BF16)
    m = jnp.dot(y, w_out[...], preferred_element_type=F32) + b_out[...]
    o_ref[0] = x + _rms(m, post_g[...])


def _lru_mixer(x, pre_g, post_g, w_in, b_in, conv_w, conv_b, w_ri, b_ri, lam, w_out, b_out):
    tok = pl.BlockSpec((1, TM, D_MODEL), lambda b, j: (b, j, 0))
    consts = (pre_g, post_g, w_in, b_in, conv_w, conv_b, w_ri, b_ri, lam, w_out, b_out)
    return pl.pallas_call(
        _lru_kernel,
        grid=(BATCH, SEQ // TM),
        in_specs=[tok] + [_const_spec(c.shape) for c in consts],
        out_specs=tok,
        out_shape=jax.ShapeDtypeStruct((BATCH, SEQ, D_MODEL), F32),
        scratch_shapes=[
            pltpu.VMEM((TM + SUBLANES, LRU_WIDTH), F32),
            pltpu.VMEM((TM, LRU_WIDTH), F32),
            pltpu.VMEM((TM, LRU_WIDTH), F32),
            pltpu.VMEM((1, LRU_WIDTH), F32),
        ],
        compiler_params=pltpu.CompilerParams(
            dimension_semantics=("arbitrary", "arbitrary"),
            vmem_limit_bytes=48 * 1024 * 1024,
        ),
        name="lru_mixer",
    )(x, *consts)


def _ffn_body(zero_ref, x, pre_g, post_g, w_up, cw, cb, w_dn, o_ref, u_scrs, tail_scr, act_scr):
    j = pl.program_id(1)
    z = zero_ref[0]

    @pl.when(j == 0)
    def _():
        tail_scr[...] = jnp.zeros(tail_scr.shape, F32)

    h = _rms(x, pre_g[...]).astype(BF16)

    def up(c):
        u_scrs[c % 2][z, SUBLANES:, :] = jnp.dot(h, w_up[c], preferred_element_type=F32)

    up(0)
    for c in range(N_FFN_CHUNKS):
        u_scr = u_scrs[c % 2]
        if c + 1 < N_FFN_CHUNKS:
            up(c + 1)
        u_scr[z, 0:SUBLANES, :] = tail_scr[c]
        tail_scr[c] = u_scr[z, TM:TM + SUBLANES, :]
        u = u_scr[z]
        cu = u * cw[c, FFN_CONV - 1:FFN_CONV, :] + cb[c]
        for k in range(1, FFN_CONV):
            cu = cu + pltpu.roll(u, k, 0) * cw[c, FFN_CONV - 1 - k:FFN_CONV - k, :]
        cu = cu[SUBLANES:, :]
        act = _twice_gelu_times(cu[:, :FFN_CHUNK], cu[:, FFN_CHUNK:])
        act_scr[:, c * FFN_CHUNK:(c + 1) * FFN_CHUNK] = act.astype(BF16)
    y = jnp.dot(act_scr[...], w_dn[...], preferred_element_type=F32)
    o_ref[0] = x + _rms(y, post_g[...])


def _ffn_kernel(zero_ref, x_ref, pre_g, post_g, w_up, cw, cb, w_dn, o_ref,
                u_scr0, u_scr1, tail_scr, act_scr):
    _ffn_body(zero_ref, x_ref[0], pre_g, post_g, w_up, cw, cb, w_dn, o_ref, (u_scr0, u_scr1),
              tail_scr, act_scr)


def _proj_ffn_kernel(zero_ref, x_ref, a_ref, w_o, mix_g, pre_g, post_g, w_up, cw, cb, w_dn, o_ref,
                     u_scr0, u_scr1, tail_scr, act_scr):
    m = jnp.dot(a_ref[0], w_o[...], preferred_element_type=F32)
    x = x_ref[0] + _rms(m, mix_g[...])
    _ffn_body(zero_ref, x, pre_g, post_g, w_up, cw, cb, w_dn, o_ref, (u_scr0, u_scr1), tail_scr,
              act_scr)


def _ffn_params(w_up, conv_w, conv_b, w_down):
    def chunked(t):
        g = t[..., :D_FF].reshape(t.shape[:-1] + (N_FFN_CHUNKS, FFN_CHUNK))
        v = t[..., D_FF:].reshape(t.shape[:-1] + (N_FFN_CHUNKS, FFN_CHUNK))
        return jnp.moveaxis(jnp.concatenate([g, v], axis=-1), -2, 0)

    return (chunked(w_up).astype(BF16), chunked(conv_w), chunked(conv_b[None, :]),
            (0.5 * w_down).astype(BF16))


def _ffn_call(kernel, token_inputs, consts):
    tok = pl.BlockSpec((1, TM, D_MODEL), lambda b, j: (b, j, 0))
    return pl.pallas_call(
        kernel,
        grid=(BATCH, SEQ // TM),
        in_specs=[pl.BlockSpec(memory_space=pltpu.SMEM)] + [tok] * len(token_inputs)
        + [_const_spec(c.shape) for c in consts],
        out_specs=tok,
        out_shape=jax.ShapeDtypeStruct((BATCH, SEQ, D_MODEL), F32),
        scratch_shapes=[pltpu.VMEM((1, SUBLANES + TM, 2 * FFN_CHUNK), F32),
                        pltpu.VMEM((1, SUBLANES + TM, 2 * FFN_CHUNK), F32),
                        pltpu.VMEM((N_FFN_CHUNKS, SUBLANES, 2 * FFN_CHUNK), F32),
                        pltpu.VMEM((TM, D_FF), BF16)],
        compiler_params=pltpu.CompilerParams(
            dimension_semantics=("arbitrary", "arbitrary"),
            vmem_limit_bytes=56 * 1024 * 1024,
        ),
        name=kernel.__name__.strip("_"),
    )(jnp.zeros((1,), jnp.int32), *token_inputs, *consts)


def _mla_proj_kernel(x_ref, ct_ref, st_ref, pre_g, w_in, q_g, w_q, kv_g, w_k, w_vt, q_feat,
                     q_ref, k_ref, vt_ref):
    h = _rms(x_ref[0], pre_g[...]).astype(BF16)
    c = jnp.dot(h, w_in[...], preferred_element_type=F32)
    qn = _rms(c[:, :Q_LORA], q_g[...]).astype(BF16)
    kvn = _rms(c[:, Q_LORA:Q_LORA + KV_LORA], kv_g[...]).astype(BF16)
    ct = ct_ref[0]
    st = st_ref[0]

    def rope(t):
        return t * ct + pltpu.roll(t, HEAD_LANES // 2, 1) * st

    kpe = rope(c[:, Q_LORA + KV_LORA:])
    q = jnp.dot(qn, w_q[...], preferred_element_type=F32)
    kn = jnp.dot(kvn, w_k[...], preferred_element_type=F32)
    q_scale = QK_DIM ** -0.5 * math.log2(math.e)
    for hd in range(MLA_HEADS):
        sl = slice(hd * HEAD_LANES, (hd + 1) * HEAD_LANES)
        q_ref[0, hd] = (rope(q[:, sl]) * q_scale + q_feat[...]).astype(BF16)
        k_ref[0, hd] = (kn[:, sl] + kpe).astype(BF16)
    vt = lax.dot_general(w_vt[...], kvn, (((1,), (1,)), ((), ())), preferred_element_type=F32)
    vt_ref[0, :, 0, 0:2 * V_DIM, :] = vt.astype(BF16).reshape(MLA_HEADS // 2, 2 * V_DIM, TM)
    vt_ref[0, :, 0, 2 * V_DIM:, :] = jnp.ones((MLA_HEADS // 2, ONES_ROWS, TM), BF16)


def _head_layout(nope, x1, x2):
    lead = nope.shape[:-1]
    z = lambda n: jnp.zeros(lead + (n,), nope.dtype)
    return jnp.concatenate([
        z(ROPE_HALF) if x1 is None else x1, nope[..., :NOPE_A],
        z(ROPE_HALF) if x2 is None else x2, nope[..., NOPE_A:],
        z(HEAD_LANES - NOPE_B_OFF - (QK_NOPE - NOPE_A)),
    ], axis=-1)


def _mla_params(w_in, w_qb, w_kvb):
    lat = Q_LORA + KV_LORA
    zpad = jnp.zeros((D_MODEL, HEAD_LANES // 2 - ROPE_HALF), w_in.dtype)
    w_in_p = jnp.concatenate([w_in[:, :lat], w_in[:, lat:lat + ROPE_HALF], zpad,
                              w_in[:, lat + ROPE_HALF:], zpad], axis=-1)
    wq = w_qb.reshape(Q_LORA, MLA_HEADS, QK_DIM)
    wq_p = _head_layout(wq[..., :QK_NOPE], wq[..., QK_NOPE:QK_NOPE + ROPE_HALF],
                        wq[..., QK_NOPE + ROPE_HALF:]).reshape(Q_LORA, MLA_HEADS * HEAD_LANES)
    wkv = w_kvb.reshape(KV_LORA, MLA_HEADS, QK_NOPE + V_DIM)
    wk_p = _head_layout(wkv[..., :QK_NOPE], None, None).reshape(KV_LORA, MLA_HEADS * HEAD_LANES)
    wv_t = wkv[..., QK_NOPE:].reshape(KV_LORA, MLA_HEADS * V_DIM).T
    return w_in_p.astype(BF16), wq_p.astype(BF16), wk_p.astype(BF16), wv_t.astype(BF16)


def _mla_proj(x, ctab, stab, pre_g, w_in_p, q_g, wq_p, kv_g, wk_p, wv_t):
    tok = pl.BlockSpec((1, TM, D_MODEL), lambda b, j: (b, j, 0))
    tab = pl.BlockSpec((1, TM, HEAD_LANES), lambda b, j: (b, j, 0))
    head = pl.BlockSpec((1, MLA_HEADS, TM, HEAD_LANES), lambda b, j: (b, 0, j, 0))
    vt_blk = pl.BlockSpec((1, MLA_HEADS // 2, 1, VT_ROWS, TM), lambda b, j: (b, 0, j, 0, 0))
    consts = (pre_g, w_in_p, q_g, wq_p, kv_g, wk_p, wv_t, _mask_features()[0])
    return pl.pallas_call(
        _mla_proj_kernel,
        grid=(BATCH, SEQ // TM),
        in_specs=[tok, tab, tab] + [_const_spec(c.shape) for c in consts],
        out_specs=(head, head, vt_blk),
        out_shape=(jax.ShapeDtypeStruct((BATCH, MLA_HEADS, SEQ, HEAD_LANES), BF16),
                   jax.ShapeDtypeStruct((BATCH, MLA_HEADS, SEQ, HEAD_LANES), BF16),
                   jax.ShapeDtypeStruct((BATCH, MLA_HEADS // 2, SEQ // TM, VT_ROWS, TM), BF16)),
        compiler_params=pltpu.CompilerParams(
            dimension_semantics=("arbitrary", "arbitrary"),
            vmem_limit_bytes=48 * 1024 * 1024,
        ),
        name="mla_proj",
    )(x, ctab, stab, *consts)


def _mask_features():
    n_chunks = TQ // CHUNK
    chunk = lax.broadcasted_iota(jnp.int32, (TQ, HEAD_LANES), 0) // CHUNK
    c = lax.broadcasted_iota(jnp.int32, (TQ, HEAD_LANES), 1) - FEAT_OFF
    lane_ok = (c >= 0) & (c < n_chunks)
    q_feat = jnp.where(lane_ok & (chunk < c), 1.0, 0.0).astype(F32)
    k_feat = jnp.where(lane_ok & (chunk == c), MASK_FEATURE, 0.0).astype(BF16)
    return q_feat, jnp.stack([jnp.zeros_like(k_feat), k_feat])


def _attn_kernel(tab_ref, q_ref, k_ref, vt_ref, kf_ref, o_ref, s_scr, acc_scr):
    nt = (((1,), (1,)), ((), ()))

    def scores(hh, t):
        q0 = pl.multiple_of(tab_ref[0, t] * TQ, TQ)
        k0 = pl.multiple_of(tab_ref[1, t] * TQ, TQ)
        k = k_ref[0, hh, pl.ds(k0, TQ), :] + kf_ref[tab_ref[2, t]]
        return lax.dot_general(k, q_ref[0, hh, pl.ds(q0, TQ), :], nt,
                               preferred_element_type=F32)

    def softmax_pv(s, vt, state, first):
        m, acc = state
        m = jnp.where(first, MASK_VALUE, m)
        m_new = jnp.maximum(m, jnp.max(s, axis=0, keepdims=True))
        alpha = jnp.exp2(m - m_new)
        p = jnp.exp2(s - m_new)
        acc = alpha * acc + jnp.dot(vt, p.astype(BF16), preferred_element_type=F32)
        return m_new, acc

    s_scr[0] = scores(0, 0)

    def step(t, state, par):
        qi = tab_ref[0, t]
        first = tab_ref[3, t] == 1
        vt = vt_ref[0, 0, tab_ref[1, t]]
        s_scr[2 + par] = scores(1, t)
        st_a = softmax_pv(s_scr[par], vt, state[0], first)
        s_scr[1 - par] = scores(0, t + 1)
        st_b = softmax_pv(s_scr[2 + par], vt, state[1], first)
        acc_scr[qi, 0] = st_a[1]
        acc_scr[qi, 1] = st_b[1]
        return st_a, st_b

    def steps(i, state):
        for k in range(ATTN_UNROLL):
            state = step(ATTN_UNROLL * i + k, state, k % 2)
        return state

    init = tuple((jnp.full((1, TQ), MASK_VALUE, F32), jnp.zeros((VT_ROWS, TQ), F32))
                 for _ in range(2))
    lax.fori_loop(0, N_ATTN_STEPS // ATTN_UNROLL, steps, init)

    first_head = lax.broadcasted_iota(jnp.int32, (2 * V_DIM, TQ), 0) < V_DIM
    for qi in range(SEQ // TQ):
        num = [acc_scr[qi, hh, 0:2 * V_DIM, :] for hh in range(2)]
        den = [acc_scr[qi, hh, 2 * V_DIM:2 * V_DIM + 1, :] for hh in range(2)]
        o_t = jnp.where(first_head, num[0] / den[0], num[1] / den[1])
        o_ref[0, qi * TQ:(qi + 1) * TQ, :] = o_t.T.astype(BF16)


def _attn_steps():
    steps = [(qi, j, int(j == qi), int(j == 0)) for qi in range(SEQ // TQ) for j in range(qi + 1)]
    steps.append(steps[-1])
    return jnp.asarray(list(zip(*steps)), dtype=jnp.int32)


def _attention(q, k, vt):
    tab, k_feat = _attn_steps(), _mask_features()[1]
    qk = pl.BlockSpec((1, 2, SEQ, HEAD_LANES), lambda b, p: (b, p, 0, 0))
    vts = pl.BlockSpec((1, 1, SEQ // TQ, VT_ROWS, TQ), lambda b, p: (b, p, 0, 0, 0))
    vo = pl.BlockSpec((1, SEQ, 2 * V_DIM), lambda b, p: (b, 0, p))
    return pl.pallas_call(
        _attn_kernel,
        grid=(BATCH, MLA_HEADS // 2),
        in_specs=[pl.BlockSpec(memory_space=pltpu.SMEM), qk, qk, vts, _const_spec(k_feat.shape)],
        out_specs=vo,
        out_shape=jax.ShapeDtypeStruct((BATCH, SEQ, MLA_HEADS * V_DIM), BF16),
        scratch_shapes=[pltpu.VMEM((4, TQ, TQ), F32),
                        pltpu.VMEM((SEQ // TQ, 2, VT_ROWS, TQ), F32)],
        compiler_params=pltpu.CompilerParams(
            dimension_semantics=("arbitrary", "arbitrary"),
            vmem_limit_bytes=48 * 1024 * 1024,
        ),
        name="chunk_causal_attention",
    )(tab, q, k, vt, k_feat)


def kernel(x, positions, mix_pre_g, mix_post_g, ffn_pre_g, ffn_post_g, lru_w_in, lru_b_in, lru_conv_w, lru_conv_b, lru_w_r, lru_b_r, lru_w_i, lru_b_i, lru_lambda, lru_w_out, lru_b_out, mla_w_in, mla_q_norm_g, mla_w_qb, mla_kv_norm_g, mla_w_kvb, mla_w_out, ffn_w_up, ffn_conv_w, ffn_conv_b, ffn_w_down):
    row = lambda t: t.reshape(1, -1)

    w_ri = jnp.concatenate([lru_w_r[0], lru_w_i[0]], axis=-1).astype(BF16)
    b_ri = jnp.concatenate([lru_b_r[0], lru_b_i[0]], axis=-1)[:, None, :]
    x = _lru_mixer(x, row(mix_pre_g[0]), row(mix_post_g[0]), lru_w_in[0].astype(BF16), row(lru_b_in[0]),
                   lru_conv_w[0], row(lru_conv_b[0]), w_ri, b_ri, row(lru_lambda[0]),
                   lru_w_out[0].astype(BF16), row(lru_b_out[0]))
    x = _ffn_call(_ffn_kernel, (x,),
                  (row(ffn_pre_g[0]), row(ffn_post_g[0]))
                  + _ffn_params(ffn_w_up[0], ffn_conv_w[0], ffn_conv_b[0], ffn_w_down[0]))

    ctab, stab = _rope_tables(positions)
    w_in_p, wq_p, wk_p, wv_t = _mla_params(mla_w_in[0], mla_w_qb[0], mla_w_kvb[0])
    q, k, vt = _mla_proj(x, ctab, stab, row(mix_pre_g[1]), w_in_p, row(mla_q_norm_g[0]), wq_p,
                         row(mla_kv_norm_g[0]), wk_p, wv_t)
    a = _attention(q, k, vt)
    x = _ffn_call(_proj_ffn_kernel, (x, a),
                  (mla_w_out[0].astype(BF16), row(mix_post_g[1]), row(ffn_pre_g[1]), row(ffn_post_g[1]))
                  + _ffn_params(ffn_w_up[1], ffn_conv_w[1], ffn_conv_b[1], ffn_w_down[1]))
    return x
```

```python
import functools
import math

import jax
import jax.numpy as jnp
from jax import lax
from jax.experimental import pallas as pl
from jax.experimental.pallas import tpu as pltpu

D_MODEL = 1024
BATCH = 4
SEQ = 4096
CHUNK = 64

LRU_WIDTH = 1024
LRU_HEADS = 4
LRU_BLOCK = LRU_WIDTH // LRU_HEADS
LRU_CONV = 4
LRU_C = 8.0

MLA_HEADS = 16
Q_LORA = 768
KV_LORA = 256
QK_NOPE = 64
QK_ROPE = 32
ROPE_HALF = QK_ROPE // 2
V_DIM = 64
QK_DIM = QK_NOPE + QK_ROPE
ROPE_THETA = 10000.0

D_FF = 2816
FFN_CONV = 3
EPS = 1e-6

LANES = 128
SUBLANES = 8
BF16_ROWS = 16

HEAD_LANES = LANES
NOPE_A = 48
X1_OFF = 0
NOPE_A_OFF = ROPE_HALF
X2_OFF = HEAD_LANES // 2
NOPE_B_OFF = X2_OFF + ROPE_HALF
FEAT_OFF = NOPE_B_OFF + (QK_NOPE - NOPE_A)

TM = 512
FFN_CHUNK = 256
N_FFN_CHUNKS = D_FF // FFN_CHUNK
TQ = 512
N_ATTN_STEPS = (SEQ // TQ) * (SEQ // TQ + 1) // 2
ATTN_UNROLL = 4
HALO = BF16_ROWS
ONES_ROWS = BF16_ROWS
VT_ROWS = 2 * V_DIM + ONES_ROWS

F32 = jnp.float32
BF16 = jnp.bfloat16
MASK_VALUE = -1e30
MASK_FEATURE = -2.0 ** 100


def _rms(x, g):
    ms = jnp.mean(x * x, axis=-1, keepdims=True)
    return x * lax.rsqrt(ms + EPS) * g


def _twice_gelu_times(x, v):
    c0 = math.sqrt(2.0 / math.pi)
    t = jnp.tanh(x * (c0 + (c0 * 0.044715) * (x * x)))
    return (x * v) * (1.0 + t)


def _const_spec(shape):
    nd = len(shape)
    return pl.BlockSpec(shape, lambda b, j: (0,) * nd, pipeline_mode=pl.Buffered(1))


def _rope_kernel(pos_ref, freq_ref, cos_ref, sin_ref):
    ang = pos_ref[...].astype(F32) * freq_ref[...]
    cos_ref[...] = jnp.cos(ang)
    sin_ref[...] = jnp.sin(ang)


def _rope_tables(positions):
    rows = BATCH * SEQ * ROPE_HALF // LANES
    pos_rep = jnp.repeat(positions.reshape(-1), ROPE_HALF).reshape(rows, LANES)
    inv_freq = 1.0 / (ROPE_THETA ** (jnp.arange(0, QK_ROPE, 2, dtype=F32) / QK_ROPE))
    freq_row = jnp.tile(inv_freq, LANES // ROPE_HALF).reshape(1, LANES)
    cos_d, sin_d = pl.pallas_call(
        _rope_kernel,
        out_shape=(jax.ShapeDtypeStruct((rows, LANES), F32),) * 2,
        name="rope_tables",
    )(pos_rep, freq_row)
    return cos_d.reshape(BATCH, SEQ, ROPE_HALF), sin_d.reshape(BATCH, SEQ, ROPE_HALF)


def _lru_kernel(zero_ref, x_ref, pre_g, post_g, w_in, b_in, conv_w, conv_b, w_ri, b_ri, lam, w_out,
                b_out, o_ref, xb_scr0, xb_scr1, tail_scr, b_scr, gate_scr, y_scr, h_scr):
    j = pl.program_id(1)
    z = zero_ref[0]

    @pl.when(j == 0)
    def _():
        tail_scr[...] = jnp.zeros(tail_scr.shape, F32)
        h_scr[...] = jnp.zeros((1, LRU_WIDTH), F32)

    x = x_ref[0]
    h = _rms(x, pre_g[...]).astype(BF16)
    xb_scrs = (xb_scr0, xb_scr1)

    def stage(hd):
        cols = slice(LRU_WIDTH + hd * LRU_BLOCK, LRU_WIDTH + (hd + 1) * LRU_BLOCK)
        xb_scrs[hd % 2][z, SUBLANES:, :] = (
            jnp.dot(h, w_in[:, cols], preferred_element_type=F32) + b_in[:, cols])

    sp = lam[...]
    neg_c_softplus = -LRU_C * (jnp.maximum(-sp, 0.0) + jnp.log(1.0 + jnp.exp(-jnp.abs(sp))))

    stage(0)
    for hd in range(LRU_HEADS):
        sl = slice(hd * LRU_BLOCK, (hd + 1) * LRU_BLOCK)
        if hd + 1 < LRU_HEADS:
            stage(hd + 1)
        gate_scr[:, sl] = jnp.dot(h, w_in[:, sl], preferred_element_type=F32) + b_in[:, sl]
        scr = xb_scrs[hd % 2]
        scr[z, 0:SUBLANES, :] = tail_scr[hd]
        tail_scr[hd] = scr[z, TM:TM + SUBLANES, :]
        xe = scr[z]
        xc = xe * conv_w[LRU_CONV - 1:LRU_CONV, sl] + conv_b[:, sl]
        for k in range(1, LRU_CONV):
            xc = xc + pltpu.roll(xe, k, 0) * conv_w[LRU_CONV - 1 - k:LRU_CONV - k, sl]
        xh = xc[SUBLANES:, :]
        zz = jnp.dot(xh.astype(BF16), w_ri[hd], preferred_element_type=F32) + b_ri[hd]
        gates = 0.5 * jnp.tanh(0.5 * zz) + 0.5
        r, ig = gates[:, :LRU_BLOCK], gates[:, LRU_BLOCK:]
        log_a = r * neg_c_softplus[:, sl]
        a = jnp.exp(log_a)
        d = 1.0 - a * a
        b = jnp.where(d > 0.0, d * lax.rsqrt(d), 0.0) * (ig * xh)
        row = lax.broadcasted_iota(jnp.int32, (SUBLANES, LRU_BLOCK), 0)
        hprev = h_scr[:, sl]
        for g in range(TM // SUBLANES):
            ag = a[g * SUBLANES:(g + 1) * SUBLANES, :]
            bg = b[g * SUBLANES:(g + 1) * SUBLANES, :]
            for s in (1, 2, 4):
                keep = row >= s
                bg = jnp.where(keep, ag * pltpu.roll(bg, s, 0) + bg, bg)
                ag = jnp.where(keep, ag * pltpu.roll(ag, s, 0), ag)
            hs = ag * hprev + bg
            hprev = hs[SUBLANES - 1:SUBLANES, :]
            b_scr[g * SUBLANES:(g + 1) * SUBLANES, sl] = hs
        h_scr[:, sl] = hprev
        y_scr[:, sl] = _twice_gelu_times(gate_scr[:, sl], b_scr[:, sl]).astype(BF16)

    m = jnp.dot(y_scr[...], w_out[...], preferred_element_type=F32) + b_out[...]
    o_ref[0] = x + _rms(m, post_g[...])


def _lru_mixer(x, pre_g, post_g, w_in, b_in, conv_w, conv_b, w_ri, b_ri, lam, w_out, b_out):
    tok = pl.BlockSpec((1, TM, D_MODEL), lambda b, j: (b, j, 0))
    consts = (pre_g, post_g, w_in, b_in, conv_w, conv_b, w_ri, b_ri, lam, w_out, b_out)
    return pl.pallas_call(
        _lru_kernel,
        grid=(BATCH, SEQ // TM),
        in_specs=[pl.BlockSpec(memory_space=pltpu.SMEM), tok] + [_const_spec(c.shape) for c in consts],
        out_specs=tok,
        out_shape=jax.ShapeDtypeStruct((BATCH, SEQ, D_MODEL), F32),
        scratch_shapes=[
            pltpu.VMEM((1, SUBLANES + TM, LRU_BLOCK), F32),
            pltpu.VMEM((1, SUBLANES + TM, LRU_BLOCK), F32),
            pltpu.VMEM((LRU_HEADS, SUBLANES, LRU_BLOCK), F32),
            pltpu.VMEM((TM, LRU_WIDTH), F32),
            pltpu.VMEM((TM, LRU_WIDTH), F32),
            pltpu.VMEM((TM, LRU_WIDTH), BF16),
            pltpu.VMEM((1, LRU_WIDTH), F32),
        ],
        compiler_params=pltpu.CompilerParams(
            dimension_semantics=("arbitrary", "arbitrary"),
            vmem_limit_bytes=48 * 1024 * 1024,
        ),
        name="lru_mixer",
    )(jnp.zeros((1,), jnp.int32), x, *consts)


def _ffn_body(zero_ref, x, pre_g, post_g, w_up, cw, cb, w_dn, o_ref, u_scrs, tail_scr, act_scr):
    j = pl.program_id(1)
    z = zero_ref[0]

    @pl.when(j == 0)
    def _():
        tail_scr[...] = jnp.zeros(tail_scr.shape, F32)

    h = _rms(x, pre_g[...]).astype(BF16)

    def up(c):
        for half, col0 in enumerate((c * FFN_CHUNK, D_FF + c * FFN_CHUNK)):
            u_scrs[c % 2][z, SUBLANES:, half * FFN_CHUNK:(half + 1) * FFN_CHUNK] = jnp.dot(
                h, w_up[:, col0:col0 + FFN_CHUNK], preferred_element_type=F32)

    up(0)
    for c in range(N_FFN_CHUNKS):
        u_scr = u_scrs[c % 2]
        if c + 1 < N_FFN_CHUNKS:
            up(c + 1)
        u_scr[z, 0:SUBLANES, :] = tail_scr[c]
        tail_scr[c] = u_scr[z, TM:TM + SUBLANES, :]
        u = u_scr[z]
        cu = u * cw[c, FFN_CONV - 1:FFN_CONV, :] + cb[c]
        for k in range(1, FFN_CONV):
            cu = cu + pltpu.roll(u, k, 0) * cw[c, FFN_CONV - 1 - k:FFN_CONV - k, :]
        cu = cu[SUBLANES:, :]
        act = _twice_gelu_times(cu[:, :FFN_CHUNK], cu[:, FFN_CHUNK:])
        act_scr[:, c * FFN_CHUNK:(c + 1) * FFN_CHUNK] = act.astype(BF16)
    y = jnp.dot(act_scr[...], w_dn[...], preferred_element_type=F32)
    o_ref[0] = x + _rms(y, post_g[...])


def _ffn_kernel(zero_ref, x_ref, pre_g, post_g, w_up, cw, cb, w_dn, o_ref,
                u_scr0, u_scr1, tail_scr, act_scr):
    _ffn_body(zero_ref, x_ref[0], pre_g, post_g, w_up, cw, cb, w_dn, o_ref, (u_scr0, u_scr1),
              tail_scr, act_scr)


def _proj_ffn_kernel(zero_ref, x_ref, a_ref, w_o, mix_g, pre_g, post_g, w_up, cw, cb, w_dn, o_ref,
                     u_scr0, u_scr1, tail_scr, act_scr):
    m = jnp.dot(a_ref[0], w_o[...], preferred_element_type=F32)
    x = x_ref[0] + _rms(m, mix_g[...])
    _ffn_body(zero_ref, x, pre_g, post_g, w_up, cw, cb, w_dn, o_ref, (u_scr0, u_scr1), tail_scr,
              act_scr)


def _ffn_params(w_up, conv_w, conv_b, w_down):
    def chunked(t):
        g = t[..., :D_FF].reshape(t.shape[:-1] + (N_FFN_CHUNKS, FFN_CHUNK))
        v = t[..., D_FF:].reshape(t.shape[:-1] + (N_FFN_CHUNKS, FFN_CHUNK))
        return jnp.moveaxis(jnp.concatenate([g, v], axis=-1), -2, 0)

    return (w_up.astype(BF16), chunked(conv_w), chunked(conv_b[None, :]),
            (0.5 * w_down).astype(BF16))


def _ffn_call(kernel, token_inputs, consts):
    tok = pl.BlockSpec((1, TM, D_MODEL), lambda b, j: (b, j, 0))
    return pl.pallas_call(
        kernel,
        grid=(BATCH, SEQ // TM),
        in_specs=[pl.BlockSpec(memory_space=pltpu.SMEM)] + [tok] * len(token_inputs)
        + [_const_spec(c.shape) for c in consts],
        out_specs=tok,
        out_shape=jax.ShapeDtypeStruct((BATCH, SEQ, D_MODEL), F32),
        scratch_shapes=[pltpu.VMEM((1, SUBLANES + TM, 2 * FFN_CHUNK), F32),
                        pltpu.VMEM((1, SUBLANES + TM, 2 * FFN_CHUNK), F32),
                        pltpu.VMEM((N_FFN_CHUNKS, SUBLANES, 2 * FFN_CHUNK), F32),
                        pltpu.VMEM((TM, D_FF), BF16)],
        compiler_params=pltpu.CompilerParams(
            dimension_semantics=("arbitrary", "arbitrary"),
            vmem_limit_bytes=56 * 1024 * 1024,
        ),
        name=kernel.__name__.strip("_"),
    )(jnp.zeros((1,), jnp.int32), *token_inputs, *consts)


def _mla_proj_kernel(x_ref, ct_ref, st_ref, pre_g, w_in, q_g, w_q, kv_g, w_k, w_vt, q_feat,
                     q_ref, k_ref, vt_ref):
    h = _rms(x_ref[0], pre_g[...]).astype(BF16)
    c = jnp.dot(h, w_in[...], preferred_element_type=F32)
    qn = _rms(c[:, :Q_LORA], q_g[...]).astype(BF16)
    kvn = _rms(c[:, Q_LORA:Q_LORA + KV_LORA], kv_g[...]).astype(BF16)
    cos, sin = ct_ref[0], st_ref[0]
    ones = jnp.ones((TM, NOPE_A), F32)
    zeros = jnp.zeros((TM, NOPE_A), F32)
    ct = jnp.concatenate([cos, ones, cos, ones], axis=1)
    st = jnp.concatenate([-sin, zeros, sin, zeros], axis=1)

    def rope(t):
        return t * ct + pltpu.roll(t, HEAD_LANES // 2, 1) * st

    kpe = rope(c[:, Q_LORA + KV_LORA:])
    q = jnp.dot(qn, w_q[...], preferred_element_type=F32)
    kn = jnp.dot(kvn, w_k[...], preferred_element_type=F32)
    q_scale = QK_DIM ** -0.5 * math.log2(math.e)
    for hd in range(MLA_HEADS):
        sl = slice(hd * HEAD_LANES, (hd + 1) * HEAD_LANES)
        q_ref[0, hd] = (rope(q[:, sl]) * q_scale + q_feat[...]).astype(BF16)
        k_ref[0, hd] = (kn[:, sl] + kpe).astype(BF16)
    vt = lax.dot_general(w_vt[...], kvn, (((1,), (1,)), ((), ())), preferred_element_type=F32)
    vt_ref[0, :, 0, 0:2 * V_DIM, :] = vt.astype(BF16).reshape(MLA_HEADS // 2, 2 * V_DIM, TM)
    vt_ref[0, :, 0, 2 * V_DIM:, :] = jnp.ones((MLA_HEADS // 2, ONES_ROWS, TM), BF16)


def _head_layout(nope, x1, x2):
    lead = nope.shape[:-1]
    z = lambda n: jnp.zeros(lead + (n,), nope.dtype)
    return jnp.concatenate([
        z(ROPE_HALF) if x1 is None else x1, nope[..., :NOPE_A],
        z(ROPE_HALF) if x2 is None else x2, nope[..., NOPE_A:],
        z(HEAD_LANES - NOPE_B_OFF - (QK_NOPE - NOPE_A)),
    ], axis=-1)


def _mla_params(w_in, w_qb, w_kvb):
    lat = Q_LORA + KV_LORA
    zpad = jnp.zeros((D_MODEL, HEAD_LANES // 2 - ROPE_HALF), w_in.dtype)
    w_in_p = jnp.concatenate([w_in[:, :lat], w_in[:, lat:lat + ROPE_HALF], zpad,
                              w_in[:, lat + ROPE_HALF:], zpad], axis=-1)
    wq = w_qb.reshape(Q_LORA, MLA_HEADS, QK_DIM)
    wq_p = _head_layout(wq[..., :QK_NOPE], wq[..., QK_NOPE:QK_NOPE + ROPE_HALF],
                        wq[..., QK_NOPE + ROPE_HALF:]).reshape(Q_LORA, MLA_HEADS * HEAD_LANES)
    wkv = w_kvb.reshape(KV_LORA, MLA_HEADS, QK_NOPE + V_DIM)
    wk_p = _head_layout(wkv[..., :QK_NOPE], None, None).reshape(KV_LORA, MLA_HEADS * HEAD_LANES)
    wv_t = wkv[..., QK_NOPE:].reshape(KV_LORA, MLA_HEADS * V_DIM).T
    return w_in_p.astype(BF16), wq_p.astype(BF16), wk_p.astype(BF16), wv_t.astype(BF16)


def _mla_proj(x, ctab, stab, pre_g, w_in_p, q_g, wq_p, kv_g, wk_p, wv_t):
    tok = pl.BlockSpec((1, TM, D_MODEL), lambda b, j: (b, j, 0))
    tab = pl.BlockSpec((1, TM, ROPE_HALF), lambda b, j: (b, j, 0))
    head = pl.BlockSpec((1, MLA_HEADS, TM, HEAD_LANES), lambda b, j: (b, 0, j, 0))
    vt_blk = pl.BlockSpec((1, MLA_HEADS // 2, 1, VT_ROWS, TM), lambda b, j: (b, 0, j, 0, 0))
    consts = (pre_g, w_in_p, q_g, wq_p, kv_g, wk_p, wv_t, _mask_features()[0])
    return pl.pallas_call(
        _mla_proj_kernel,
        grid=(BATCH, SEQ // TM),
        in_specs=[tok, tab, tab] + [_const_spec(c.shape) for c in consts],
        out_specs=(head, head, vt_blk),
        out_shape=(jax.ShapeDtypeStruct((BATCH, MLA_HEADS, SEQ, HEAD_LANES), BF16),
                   jax.ShapeDtypeStruct((BATCH, MLA_HEADS, SEQ, HEAD_LANES), BF16),
                   jax.ShapeDtypeStruct((BATCH, MLA_HEADS // 2, SEQ // TM, VT_ROWS, TM), BF16)),
        compiler_params=pltpu.CompilerParams(
            dimension_semantics=("arbitrary", "arbitrary"),
            vmem_limit_bytes=48 * 1024 * 1024,
        ),
        name="mla_proj",
    )(x, ctab, stab, *consts)


def _mask_features():
    n_chunks = TQ // CHUNK
    chunk = lax.broadcasted_iota(jnp.int32, (TQ, HEAD_LANES), 0) // CHUNK
    c = lax.broadcasted_iota(jnp.int32, (TQ, HEAD_LANES), 1) - FEAT_OFF
    lane_ok = (c >= 0) & (c < n_chunks)
    q_feat = jnp.where(lane_ok & (chunk < c), 1.0, 0.0).astype(F32)
    k_feat = jnp.where(lane_ok & (chunk == c), MASK_FEATURE, 0.0).astype(BF16)
    return q_feat, jnp.stack([jnp.zeros_like(k_feat), k_feat])


def _attn_kernel(tab_ref, q_ref, k_ref, vt_ref, kf_ref, o_ref, s_scr, acc_scr):
    nt = (((1,), (1,)), ((), ()))

    def scores(hh, t):
        q0 = pl.multiple_of(tab_ref[0, t] * TQ, TQ)
        k0 = pl.multiple_of(tab_ref[1, t] * TQ, TQ)
        k = k_ref[0, hh, pl.ds(k0, TQ), :] + kf_ref[tab_ref[2, t]]
        return lax.dot_general(k, q_ref[0, hh, pl.ds(q0, TQ), :], nt,
                               preferred_element_type=F32)

    def softmax_pv(s, vt, state, first):
        m, acc = state
        m = jnp.where(first, MASK_VALUE, m)
        m_new = jnp.maximum(m, jnp.max(s, axis=0, keepdims=True))
        alpha = jnp.exp2(m - m_new)
        p = jnp.exp2(s - m_new)
        acc = alpha * acc + jnp.dot(vt, p.astype(BF16), preferred_element_type=F32)
        return m_new, acc

    s_scr[0] = scores(0, 0)

    def step(t, state, par):
        qi = tab_ref[0, t]
        first = tab_ref[3, t] == 1
        vt = vt_ref[0, 0, tab_ref[1, t]]
        s_scr[2 + par] = scores(1, t)
        st_a = softmax_pv(s_scr[par], vt, state[0], first)
        s_scr[1 - par] = scores(0, t + 1)
        st_b = softmax_pv(s_scr[2 + par], vt, state[1], first)
        acc_scr[qi, 0] = st_a[1]
        acc_scr[qi, 1] = st_b[1]
        return st_a, st_b

    def steps(i, state):
        for k in range(ATTN_UNROLL):
            state = step(ATTN_UNROLL * i + k, state, k % 2)
        return state

    init = tuple((jnp.full((1, TQ), MASK_VALUE, F32), jnp.zeros((VT_ROWS, TQ), F32))
                 for _ in range(2))
    lax.fori_loop(0, N_ATTN_STEPS // ATTN_UNROLL, steps, init)

    first_head = lax.broadcasted_iota(jnp.int32, (2 * V_DIM, TQ), 0) < V_DIM
    for qi in range(SEQ // TQ):
        num = [acc_scr[qi, hh, 0:2 * V_DIM, :] for hh in range(2)]
        den = [acc_scr[qi, hh, 2 * V_DIM:2 * V_DIM + 1, :] for hh in range(2)]
        o_t = jnp.where(first_head, num[0] / den[0], num[1] / den[1])
        o_ref[0, qi * TQ:(qi + 1) * TQ, :] = o_t.T.astype(BF16)


def _attn_steps():
    steps = [(qi, j, int(j == qi), int(j == 0)) for qi in range(SEQ // TQ) for j in range(qi + 1)]
    steps.append(steps[-1])
    return jnp.asarray(list(zip(*steps)), dtype=jnp.int32)


def _attention(q, k, vt):
    tab, k_feat = _attn_steps(), _mask_features()[1]
    qk = pl.BlockSpec((1, 2, SEQ, HEAD_LANES), lambda b, p: (b, p, 0, 0))
    vts = pl.BlockSpec((1, 1, SEQ // TQ, VT_ROWS, TQ), lambda b, p: (b, p, 0, 0, 0))
    vo = pl.BlockSpec((1, SEQ, 2 * V_DIM), lambda b, p: (b, 0, p))
    return pl.pallas_call(
        _attn_kernel,
        grid=(BATCH, MLA_HEADS // 2),
        in_specs=[pl.BlockSpec(memory_space=pltpu.SMEM), qk, qk, vts, _const_spec(k_feat.shape)],
        out_specs=vo,
        out_shape=jax.ShapeDtypeStruct((BATCH, SEQ, MLA_HEADS * V_DIM), BF16),
        scratch_shapes=[pltpu.VMEM((4, TQ, TQ), F32),
                        pltpu.VMEM((SEQ // TQ, 2, VT_ROWS, TQ), F32)],
        compiler_params=pltpu.CompilerParams(
            dimension_semantics=("arbitrary", "arbitrary"),
            vmem_limit_bytes=48 * 1024 * 1024,
        ),
        name="chunk_causal_attention",
    )(tab, q, k, vt, k_feat)


def kernel(x, positions, mix_pre_g, mix_post_g, ffn_pre_g, ffn_post_g, lru_w_in, lru_b_in, lru_conv_w, lru_conv_b, lru_w_r, lru_b_r, lru_w_i, lru_b_i, lru_lambda, lru_w_out, lru_b_out, mla_w_in, mla_q_norm_g, mla_w_qb, mla_kv_norm_g, mla_w_kvb, mla_w_out, ffn_w_up, ffn_conv_w, ffn_conv_b, ffn_w_down):
    row = lambda t: t.reshape(1, -1)

    w_ri = jnp.concatenate([lru_w_r[0], lru_w_i[0]], axis=-1).astype(BF16)
    b_ri = jnp.concatenate([lru_b_r[0], lru_b_i[0]], axis=-1)[:, None, :]
    x = _lru_mixer(x, row(mix_pre_g[0]), row(mix_post_g[0]), lru_w_in[0].astype(BF16), row(lru_b_in[0]),
                   lru_conv_w[0], row(lru_conv_b[0]), w_ri, b_ri, row(lru_lambda[0]),
                   (0.5 * lru_w_out[0]).astype(BF16), row(lru_b_out[0]))
    x = _ffn_call(_ffn_kernel, (x,),
                  (row(ffn_pre_g[0]), row(ffn_post_g[0]))
                  + _ffn_params(ffn_w_up[0], ffn_conv_w[0], ffn_conv_b[0], ffn_w_down[0]))

    ctab, stab = _rope_tables(positions)
    w_in_p, wq_p, wk_p, wv_t = _mla_params(mla_w_in[0], mla_w_qb[0], mla_w_kvb[0])
    q, k, vt = _mla_proj(x, ctab, stab, row(mix_pre_g[1]), w_in_p, row(mla_q_norm_g[0]), wq_p,
                         row(mla_kv_norm_g[0]), wk_p, wv_t)
    a = _attention(q, k, vt)
    x = _ffn_call(_proj_ffn_kernel, (x, a),
                  (mla_w_out[0].astype(BF16), row(mix_post_g[1]), row(ffn_pre_g[1]), row(ffn_post_g[1]))
                  + _ffn_params(ffn_w_up[1], ffn_conv_w[1], ffn_conv_b[1], ffn_w_down[1]))
    return x
```

```python
import functools
import math

import jax
import jax.numpy as jnp
from jax import lax
from jax.experimental import pallas as pl
from jax.experimental.pallas import tpu as pltpu

D_MODEL = 1024
BATCH = 4
SEQ = 4096
CHUNK = 64

LRU_WIDTH = 1024
LRU_HEADS = 4
LRU_BLOCK = LRU_WIDTH // LRU_HEADS
LRU_CONV = 4
LRU_C = 8.0

MLA_HEADS = 16
Q_LORA = 768
KV_LORA = 256
QK_NOPE = 64
QK_ROPE = 32
ROPE_HALF = QK_ROPE // 2
V_DIM = 64
QK_DIM = QK_NOPE + QK_ROPE
ROPE_THETA = 10000.0

D_FF = 2816
FFN_CONV = 3
EPS = 1e-6

LANES = 128
SUBLANES = 8
BF16_ROWS = 16

HEAD_LANES = LANES
NOPE_A = 48
X1_OFF = 0
NOPE_A_OFF = ROPE_HALF
X2_OFF = HEAD_LANES // 2
NOPE_B_OFF = X2_OFF + ROPE_HALF
FEAT_OFF = NOPE_B_OFF + (QK_NOPE - NOPE_A)

TM = 512
FFN_CHUNK = 256
N_FFN_CHUNKS = D_FF // FFN_CHUNK
TQ = 512
N_ATTN_STEPS = (SEQ // TQ) * (SEQ // TQ + 1) // 2
ATTN_UNROLL = 6
HALO = BF16_ROWS
ONES_ROWS = BF16_ROWS
VT_ROWS = 2 * V_DIM + ONES_ROWS

F32 = jnp.float32
BF16 = jnp.bfloat16
MASK_VALUE = -1e30
MASK_FEATURE = -2.0 ** 100


def _rms(x, g):
    ms = jnp.mean(x * x, axis=-1, keepdims=True)
    return x * lax.rsqrt(ms + EPS) * g


def _twice_gelu_times(x, v):
    c0 = math.sqrt(2.0 / math.pi)
    t = jnp.tanh(x * (c0 + (c0 * 0.044715) * (x * x)))
    return (x * v) * (1.0 + t)


def _const_spec(shape):
    nd = len(shape)
    return pl.BlockSpec(shape, lambda b, j: (0,) * nd, pipeline_mode=pl.Buffered(1))


def _rope_kernel(pos_ref, freq_ref, cos_ref, sin_ref):
    ang = pos_ref[...].astype(F32) * freq_ref[...]
    cos_ref[...] = jnp.cos(ang)
    sin_ref[...] = jnp.sin(ang)


def _rope_tables(positions):
    rows = BATCH * SEQ * ROPE_HALF // LANES
    pos_rep = jnp.repeat(positions.reshape(-1), ROPE_HALF).reshape(rows, LANES)
    inv_freq = 1.0 / (ROPE_THETA ** (jnp.arange(0, QK_ROPE, 2, dtype=F32) / QK_ROPE))
    freq_row = jnp.tile(inv_freq, LANES // ROPE_HALF).reshape(1, LANES)
    cos_d, sin_d = pl.pallas_call(
        _rope_kernel,
        out_shape=(jax.ShapeDtypeStruct((rows, LANES), F32),) * 2,
        name="rope_tables",
    )(pos_rep, freq_row)
    return cos_d.reshape(BATCH, SEQ, ROPE_HALF), sin_d.reshape(BATCH, SEQ, ROPE_HALF)


def _lru_kernel(zero_ref, x_ref, pre_g, post_g, w_in, b_in, conv_w, conv_b, w_ri, b_ri, lam, w_out,
                b_out, o_ref, xb_scr0, xb_scr1, tail_scr, b_scr, gate_scr, y_scr, h_scr):
    j = pl.program_id(1)
    z = zero_ref[0]

    @pl.when(j == 0)
    def _():
        tail_scr[...] = jnp.zeros(tail_scr.shape, F32)
        h_scr[...] = jnp.zeros((1, LRU_WIDTH), F32)

    x = x_ref[0]
    h = _rms(x, pre_g[...]).astype(BF16)
    xb_scrs = (xb_scr0, xb_scr1)

    def stage(hd):
        cols = slice(LRU_WIDTH + hd * LRU_BLOCK, LRU_WIDTH + (hd + 1) * LRU_BLOCK)
        xb_scrs[hd % 2][z, SUBLANES:, :] = (
            jnp.dot(h, w_in[:, cols], preferred_element_type=F32) + b_in[:, cols])

    sp = lam[...]
    neg_c_softplus = -LRU_C * (jnp.maximum(-sp, 0.0) + jnp.log(1.0 + jnp.exp(-jnp.abs(sp))))

    stage(0)
    for hd in range(LRU_HEADS):
        sl = slice(hd * LRU_BLOCK, (hd + 1) * LRU_BLOCK)
        if hd + 1 < LRU_HEADS:
            stage(hd + 1)
        gate_scr[:, sl] = jnp.dot(h, w_in[:, sl], preferred_element_type=F32) + b_in[:, sl]
        scr = xb_scrs[hd % 2]
        scr[z, 0:SUBLANES, :] = tail_scr[hd]
        tail_scr[hd] = scr[z, TM:TM + SUBLANES, :]
        xe = scr[z]
        xc = xe * conv_w[LRU_CONV - 1:LRU_CONV, sl] + conv_b[:, sl]
        for k in range(1, LRU_CONV):
            xc = xc + pltpu.roll(xe, k, 0) * conv_w[LRU_CONV - 1 - k:LRU_CONV - k, sl]
        xh = xc[SUBLANES:, :]
        zz = jnp.dot(xh.astype(BF16), w_ri[hd], preferred_element_type=F32) + b_ri[hd]
        gates = 0.5 * jnp.tanh(0.5 * zz) + 0.5
        r, ig = gates[:, :LRU_BLOCK], gates[:, LRU_BLOCK:]
        log_a = r * neg_c_softplus[:, sl]
        a = jnp.exp(log_a)
        d = 1.0 - a * a
        b = jnp.where(d > 0.0, d * lax.rsqrt(d), 0.0) * (ig * xh)
        row = lax.broadcasted_iota(jnp.int32, (SUBLANES, LRU_BLOCK), 0)
        hprev = h_scr[:, sl]
        for g in range(TM // SUBLANES):
            ag = a[g * SUBLANES:(g + 1) * SUBLANES, :]
            bg = b[g * SUBLANES:(g + 1) * SUBLANES, :]
            for s in (1, 2, 4):
                keep = row >= s
                bg = jnp.where(keep, ag * pltpu.roll(bg, s, 0) + bg, bg)
                ag = jnp.where(keep, ag * pltpu.roll(ag, s, 0), ag)
            hs = ag * hprev + bg
            hprev = hs[SUBLANES - 1:SUBLANES, :]
            b_scr[g * SUBLANES:(g + 1) * SUBLANES, sl] = hs
        h_scr[:, sl] = hprev
        y_scr[:, sl] = _twice_gelu_times(gate_scr[:, sl], b_scr[:, sl]).astype(BF16)

    m = jnp.dot(y_scr[...], w_out[...], preferred_element_type=F32) + b_out[...]
    o_ref[0] = x + _rms(m, post_g[...])


def _lru_mixer(x, pre_g, post_g, w_in, b_in, conv_w, conv_b, w_ri, b_ri, lam, w_out, b_out):
    tok = pl.BlockSpec((1, TM, D_MODEL), lambda b, j: (b, j, 0))
    consts = (pre_g, post_g, w_in, b_in, conv_w, conv_b, w_ri, b_ri, lam, w_out, b_out)
    return pl.pallas_call(
        _lru_kernel,
        grid=(BATCH, SEQ // TM),
        in_specs=[pl.BlockSpec(memory_space=pltpu.SMEM), tok] + [_const_spec(c.shape) for c in consts],
        out_specs=tok,
        out_shape=jax.ShapeDtypeStruct((BATCH, SEQ, D_MODEL), F32),
        scratch_shapes=[
            pltpu.VMEM((1, SUBLANES + TM, LRU_BLOCK), F32),
            pltpu.VMEM((1, SUBLANES + TM, LRU_BLOCK), F32),
            pltpu.VMEM((LRU_HEADS, SUBLANES, LRU_BLOCK), F32),
            pltpu.VMEM((TM, LRU_WIDTH), F32),
            pltpu.VMEM((TM, LRU_WIDTH), F32),
            pltpu.VMEM((TM, LRU_WIDTH), BF16),
            pltpu.VMEM((1, LRU_WIDTH), F32),
        ],
        compiler_params=pltpu.CompilerParams(
            dimension_semantics=("arbitrary", "arbitrary"),
            vmem_limit_bytes=48 * 1024 * 1024,
        ),
        name="lru_mixer",
    )(jnp.zeros((1,), jnp.int32), x, *consts)


def _ffn_body(zero_ref, x, pre_g, post_g, w_up, cw, cb, w_dn, o_ref, u_scrs, tail_scr, act_scr):
    j = pl.program_id(1)
    z = zero_ref[0]

    @pl.when(j == 0)
    def _():
        tail_scr[...] = jnp.zeros(tail_scr.shape, F32)

    h = _rms(x, pre_g[...]).astype(BF16)

    def up(c):
        for half, col0 in enumerate((c * FFN_CHUNK, D_FF + c * FFN_CHUNK)):
            u_scrs[c % 2][z, SUBLANES:, half * FFN_CHUNK:(half + 1) * FFN_CHUNK] = jnp.dot(
                h, w_up[:, col0:col0 + FFN_CHUNK], preferred_element_type=F32)

    up(0)
    for c in range(N_FFN_CHUNKS):
        u_scr = u_scrs[c % 2]
        if c + 1 < N_FFN_CHUNKS:
            up(c + 1)
        u_scr[z, 0:SUBLANES, :] = tail_scr[c]
        tail_scr[c] = u_scr[z, TM:TM + SUBLANES, :]
        u = u_scr[z]
        cu = u * cw[c, FFN_CONV - 1:FFN_CONV, :] + cb[c]
        for k in range(1, FFN_CONV):
            cu = cu + pltpu.roll(u, k, 0) * cw[c, FFN_CONV - 1 - k:FFN_CONV - k, :]
        cu = cu[SUBLANES:, :]
        act = _twice_gelu_times(cu[:, :FFN_CHUNK], cu[:, FFN_CHUNK:])
        act_scr[:, c * FFN_CHUNK:(c + 1) * FFN_CHUNK] = act.astype(BF16)
    y = jnp.dot(act_scr[...], w_dn[...], preferred_element_type=F32)
    o_ref[0] = x + _rms(y, post_g[...])


def _ffn_kernel(zero_ref, x_ref, pre_g, post_g, w_up, cw, cb, w_dn, o_ref,
                u_scr0, u_scr1, tail_scr, act_scr):
    _ffn_body(zero_ref, x_ref[0], pre_g, post_g, w_up, cw, cb, w_dn, o_ref, (u_scr0, u_scr1),
              tail_scr, act_scr)


def _proj_ffn_kernel(zero_ref, x_ref, a_ref, w_o, mix_g, pre_g, post_g, w_up, cw, cb, w_dn, o_ref,
                     u_scr0, u_scr1, tail_scr, act_scr):
    m = jnp.dot(a_ref[0], w_o[...], preferred_element_type=F32)
    x = x_ref[0] + _rms(m, mix_g[...])
    _ffn_body(zero_ref, x, pre_g, post_g, w_up, cw, cb, w_dn, o_ref, (u_scr0, u_scr1), tail_scr,
              act_scr)


def _ffn_params(w_up, conv_w, conv_b, w_down):
    def chunked(t):
        g = t[..., :D_FF].reshape(t.shape[:-1] + (N_FFN_CHUNKS, FFN_CHUNK))
        v = t[..., D_FF:].reshape(t.shape[:-1] + (N_FFN_CHUNKS, FFN_CHUNK))
        return jnp.moveaxis(jnp.concatenate([g, v], axis=-1), -2, 0)

    return (w_up.astype(BF16), chunked(conv_w), chunked(conv_b[None, :]),
            (0.5 * w_down).astype(BF16))


def _ffn_call(kernel, token_inputs, consts):
    tok = pl.BlockSpec((1, TM, D_MODEL), lambda b, j: (b, j, 0))
    return pl.pallas_call(
        kernel,
        grid=(BATCH, SEQ // TM),
        in_specs=[pl.BlockSpec(memory_space=pltpu.SMEM)] + [tok] * len(token_inputs)
        + [_const_spec(c.shape) for c in consts],
        out_specs=tok,
        out_shape=jax.ShapeDtypeStruct((BATCH, SEQ, D_MODEL), F32),
        scratch_shapes=[pltpu.VMEM((1, SUBLANES + TM, 2 * FFN_CHUNK), F32),
                        pltpu.VMEM((1, SUBLANES + TM, 2 * FFN_CHUNK), F32),
                        pltpu.VMEM((N_FFN_CHUNKS, SUBLANES, 2 * FFN_CHUNK), F32),
                        pltpu.VMEM((TM, D_FF), BF16)],
        compiler_params=pltpu.CompilerParams(
            dimension_semantics=("arbitrary", "arbitrary"),
            vmem_limit_bytes=56 * 1024 * 1024,
        ),
        name=kernel.__name__.strip("_"),
    )(jnp.zeros((1,), jnp.int32), *token_inputs, *consts)


def _mla_proj_kernel(x_ref, ct_ref, st_ref, pre_g, w_in, q_g, w_q, kv_g, w_k, w_vt, q_feat,
                     q_ref, k_ref, vt_ref):
    h = _rms(x_ref[0], pre_g[...]).astype(BF16)
    c = jnp.dot(h, w_in[...], preferred_element_type=F32)
    qn = _rms(c[:, :Q_LORA], q_g[...]).astype(BF16)
    kvn = _rms(c[:, Q_LORA:Q_LORA + KV_LORA], kv_g[...]).astype(BF16)
    cos, sin = ct_ref[0], st_ref[0]
    ones = jnp.ones((TM, NOPE_A), F32)
    zeros = jnp.zeros((TM, NOPE_A), F32)
    ct = jnp.concatenate([cos, ones, cos, ones], axis=1)
    st = jnp.concatenate([-sin, zeros, sin, zeros], axis=1)

    def rope(t):
        return t * ct + pltpu.roll(t, HEAD_LANES // 2, 1) * st

    kpe = rope(c[:, Q_LORA + KV_LORA:])
    q = jnp.dot(qn, w_q[...], preferred_element_type=F32)
    kn = jnp.dot(kvn, w_k[...], preferred_element_type=F32)
    q_scale = QK_DIM ** -0.5 * math.log2(math.e)
    for hd in range(MLA_HEADS):
        sl = slice(hd * HEAD_LANES, (hd + 1) * HEAD_LANES)
        q_ref[0, hd] = (rope(q[:, sl]) * q_scale + q_feat[...]).astype(BF16)
        k_ref[0, hd] = (kn[:, sl] + kpe).astype(BF16)
    vt = lax.dot_general(w_vt[...], kvn, (((1,), (1,)), ((), ())), preferred_element_type=F32)
    vt_ref[0, :, 0, 0:2 * V_DIM, :] = vt.astype(BF16).reshape(MLA_HEADS // 2, 2 * V_DIM, TM)
    vt_ref[0, :, 0, 2 * V_DIM:, :] = jnp.ones((MLA_HEADS // 2, ONES_ROWS, TM), BF16)


def _head_layout(nope, x1, x2):
    lead = nope.shape[:-1]
    z = lambda n: jnp.zeros(lead + (n,), nope.dtype)
    return jnp.concatenate([
        z(ROPE_HALF) if x1 is None else x1, nope[..., :NOPE_A],
        z(ROPE_HALF) if x2 is None else x2, nope[..., NOPE_A:],
        z(HEAD_LANES - NOPE_B_OFF - (QK_NOPE - NOPE_A)),
    ], axis=-1)


def _mla_params(w_in, w_qb, w_kvb):
    lat = Q_LORA + KV_LORA
    zpad = jnp.zeros((D_MODEL, HEAD_LANES // 2 - ROPE_HALF), w_in.dtype)
    w_in_p = jnp.concatenate([w_in[:, :lat], w_in[:, lat:lat + ROPE_HALF], zpad,
                              w_in[:, lat + ROPE_HALF:], zpad], axis=-1)
    wq = w_qb.reshape(Q_LORA, MLA_HEADS, QK_DIM)
    wq_p = _head_layout(wq[..., :QK_NOPE], wq[..., QK_NOPE:QK_NOPE + ROPE_HALF],
                        wq[..., QK_NOPE + ROPE_HALF:]).reshape(Q_LORA, MLA_HEADS * HEAD_LANES)
    wkv = w_kvb.reshape(KV_LORA, MLA_HEADS, QK_NOPE + V_DIM)
    wk_p = _head_layout(wkv[..., :QK_NOPE], None, None).reshape(KV_LORA, MLA_HEADS * HEAD_LANES)
    wv_t = wkv[..., QK_NOPE:].reshape(KV_LORA, MLA_HEADS * V_DIM).T
    return w_in_p.astype(BF16), wq_p.astype(BF16), wk_p.astype(BF16), wv_t.astype(BF16)


def _mla_proj(x, ctab, stab, pre_g, w_in_p, q_g, wq_p, kv_g, wk_p, wv_t):
    tok = pl.BlockSpec((1, TM, D_MODEL), lambda b, j: (b, j, 0))
    tab = pl.BlockSpec((1, TM, ROPE_HALF), lambda b, j: (b, j, 0))
    head = pl.BlockSpec((1, MLA_HEADS, TM, HEAD_LANES), lambda b, j: (b, 0, j, 0))
    vt_blk = pl.BlockSpec((1, MLA_HEADS // 2, 1, VT_ROWS, TM), lambda b, j: (b, 0, j, 0, 0))
    consts = (pre_g, w_in_p, q_g, wq_p, kv_g, wk_p, wv_t, _mask_features()[0])
    return pl.pallas_call(
        _mla_proj_kernel,
        grid=(BATCH, SEQ // TM),
        in_specs=[tok, tab, tab] + [_const_spec(c.shape) for c in consts],
        out_specs=(head, head, vt_blk),
        out_shape=(jax.ShapeDtypeStruct((BATCH, MLA_HEADS, SEQ, HEAD_LANES), BF16),
                   jax.ShapeDtypeStruct((BATCH, MLA_HEADS, SEQ, HEAD_LANES), BF16),
                   jax.ShapeDtypeStruct((BATCH, MLA_HEADS // 2, SEQ // TM, VT_ROWS, TM), BF16)),
        compiler_params=pltpu.CompilerParams(
            dimension_semantics=("arbitrary", "arbitrary"),
            vmem_limit_bytes=48 * 1024 * 1024,
        ),
        name="mla_proj",
    )(x, ctab, stab, *consts)


def _mask_features():
    n_chunks = TQ // CHUNK
    chunk = lax.broadcasted_iota(jnp.int32, (TQ, HEAD_LANES), 0) // CHUNK
    c = lax.broadcasted_iota(jnp.int32, (TQ, HEAD_LANES), 1) - FEAT_OFF
    lane_ok = (c >= 0) & (c < n_chunks)
    q_feat = jnp.where(lane_ok & (chunk < c), 1.0, 0.0).astype(F32)
    k_feat = jnp.where(lane_ok & (chunk == c), MASK_FEATURE, 0.0).astype(BF16)
    return q_feat, jnp.stack([jnp.zeros_like(k_feat), k_feat])


def _attn_kernel(tab_ref, q_ref, k_ref, vt_ref, kf_ref, o_ref, s_scr, acc_scr):
    nt = (((1,), (1,)), ((), ()))

    def scores(hh, t):
        q0 = pl.multiple_of(tab_ref[0, t] * TQ, TQ)
        k0 = pl.multiple_of(tab_ref[1, t] * TQ, TQ)
        k = k_ref[0, hh, pl.ds(k0, TQ), :] + kf_ref[tab_ref[2, t]]
        return lax.dot_general(k, q_ref[0, hh, pl.ds(q0, TQ), :], nt,
                               preferred_element_type=F32)

    def softmax_pv(s, vt, state, first):
        m, acc = state
        m = jnp.where(first, MASK_VALUE, m)
        m_new = jnp.maximum(m, jnp.max(s, axis=0, keepdims=True))
        alpha = jnp.exp2(m - m_new)
        p = jnp.exp2(s - m_new)
        acc = alpha * acc + jnp.dot(vt, p.astype(BF16), preferred_element_type=F32)
        return m_new, acc

    s_scr[0] = scores(0, 0)

    def step(t, state, par):
        qi = tab_ref[0, t]
        first = tab_ref[3, t] == 1
        vt = vt_ref[0, 0, tab_ref[1, t]]
        s_scr[2 + par] = scores(1, t)
        st_a = softmax_pv(s_scr[par], vt, state[0], first)
        s_scr[1 - par] = scores(0, t + 1)
        st_b = softmax_pv(s_scr[2 + par], vt, state[1], first)
        acc_scr[qi, 0] = st_a[1]
        acc_scr[qi, 1] = st_b[1]
        return st_a, st_b

    def steps(i, state):
        for k in range(ATTN_UNROLL):
            state = step(ATTN_UNROLL * i + k, state, k % 2)
        return state

    init = tuple((jnp.full((1, TQ), MASK_VALUE, F32), jnp.zeros((VT_ROWS, TQ), F32))
                 for _ in range(2))
    lax.fori_loop(0, N_ATTN_STEPS // ATTN_UNROLL, steps, init)

    first_head = lax.broadcasted_iota(jnp.int32, (2 * V_DIM, TQ), 0) < V_DIM
    for qi in range(SEQ // TQ):
        num = [acc_scr[qi, hh, 0:2 * V_DIM, :] for hh in range(2)]
        den = [acc_scr[qi, hh, 2 * V_DIM:2 * V_DIM + 1, :] for hh in range(2)]
        o_t = jnp.where(first_head, num[0] / den[0], num[1] / den[1])
        o_ref[0, qi * TQ:(qi + 1) * TQ, :] = o_t.T.astype(BF16)


def _attn_steps():
    steps = [(qi, j, int(j == qi), int(j == 0)) for qi in range(SEQ // TQ) for j in range(qi + 1)]
    steps.append(steps[-1])
    return jnp.asarray(list(zip(*steps)), dtype=jnp.int32)


def _attention(q, k, vt):
    tab, k_feat = _attn_steps(), _mask_features()[1]
    qk = pl.BlockSpec((1, 2, SEQ, HEAD_LANES), lambda b, p: (b, p, 0, 0))
    vts = pl.BlockSpec((1, 1, SEQ // TQ, VT_ROWS, TQ), lambda b, p: (b, p, 0, 0, 0))
    vo = pl.BlockSpec((1, SEQ, 2 * V_DIM), lambda b, p: (b, 0, p))
    return pl.pallas_call(
        _attn_kernel,
        grid=(BATCH, MLA_HEADS // 2),
        in_specs=[pl.BlockSpec(memory_space=pltpu.SMEM), qk, qk, vts, _const_spec(k_feat.shape)],
        out_specs=vo,
        out_shape=jax.ShapeDtypeStruct((BATCH, SEQ, MLA_HEADS * V_DIM), BF16),
        scratch_shapes=[pltpu.VMEM((4, TQ, TQ), F32),
                        pltpu.VMEM((SEQ // TQ, 2, VT_ROWS, TQ), F32)],
        compiler_params=pltpu.CompilerParams(
            dimension_semantics=("arbitrary", "arbitrary"),
            vmem_limit_bytes=48 * 1024 * 1024,
        ),
        name="chunk_causal_attention",
    )(tab, q, k, vt, k_feat)


def kernel(x, positions, mix_pre_g, mix_post_g, ffn_pre_g, ffn_post_g, lru_w_in, lru_b_in, lru_conv_w, lru_conv_b, lru_w_r, lru_b_r, lru_w_i, lru_b_i, lru_lambda, lru_w_out, lru_b_out, mla_w_in, mla_q_norm_g, mla_w_qb, mla_kv_norm_g, mla_w_kvb, mla_w_out, ffn_w_up, ffn_conv_w, ffn_conv_b, ffn_w_down):
    row = lambda t: t.reshape(1, -1)

    w_ri = jnp.concatenate([lru_w_r[0], lru_w_i[0]], axis=-1).astype(BF16)
    b_ri = jnp.concatenate([lru_b_r[0], lru_b_i[0]], axis=-1)[:, None, :]
    x = _lru_mixer(x, row(mix_pre_g[0]), row(mix_post_g[0]), lru_w_in[0].astype(BF16), row(lru_b_in[0]),
                   lru_conv_w[0], row(lru_conv_b[0]), w_ri, b_ri, row(lru_lambda[0]),
                   (0.5 * lru_w_out[0]).astype(BF16), row(lru_b_out[0]))
    x = _ffn_call(_ffn_kernel, (x,),
                  (row(ffn_pre_g[0]), row(ffn_post_g[0]))
                  + _ffn_params(ffn_w_up[0], ffn_conv_w[0], ffn_conv_b[0], ffn_w_down[0]))

    ctab, stab = _rope_tables(positions)
    w_in_p, wq_p, wk_p, wv_t = _mla_params(mla_w_in[0], mla_w_qb[0], mla_w_kvb[0])
    q, k, vt = _mla_proj(x, ctab, stab, row(mix_pre_g[1]), w_in_p, row(mla_q_norm_g[0]), wq_p,
                         row(mla_kv_norm_g[0]), wk_p, wv_t)
    a = _attention(q, k, vt)
    x = _ffn_call(_proj_ffn_kernel, (x, a),
                  (mla_w_out[0].astype(BF16), row(mix_post_g[1]), row(ffn_pre_g[1]), row(ffn_post_g[1]))
                  + _ffn_params(ffn_w_up[1], ffn_conv_w[1], ffn_conv_b[1], ffn_w_down[1]))
    return x
```

```python
import math

import jax
import jax.numpy as jnp
from jax import lax
from jax.experimental import pallas as pl
from jax.experimental.pallas import tpu as pltpu

D_MODEL = 1024
BATCH = 4
SEQ = 4096
CHUNK = 64

LRU_WIDTH = 1024
LRU_HEADS = 4
LRU_BLOCK = LRU_WIDTH // LRU_HEADS
LRU_CONV = 4
LRU_C = 8.0

MLA_HEADS = 16
Q_LORA = 768
KV_LORA = 256
QK_NOPE = 64
QK_ROPE = 32
ROPE_HALF = QK_ROPE // 2
V_DIM = 64
QK_DIM = QK_NOPE + QK_ROPE
ROPE_THETA = 10000.0

D_FF = 2816
FFN_CONV = 3
EPS = 1e-6

LANES = 128
SUBLANES = 8
BF16_ROWS = 16

HEAD_LANES = LANES
NOPE_A = 48
X2_OFF = HEAD_LANES // 2
NOPE_B_OFF = X2_OFF + ROPE_HALF
FEAT_OFF = NOPE_B_OFF + (QK_NOPE - NOPE_A)

TM = 512
FFN_CHUNK = 256
N_FFN_CHUNKS = D_FF // FFN_CHUNK
TQ = 512
N_ATTN_STEPS = (SEQ // TQ) * (SEQ // TQ + 1) // 2
ATTN_UNROLL = 12
ONES_ROWS = BF16_ROWS
VT_ROWS = 2 * V_DIM + ONES_ROWS

MIB = 1024 * 1024
VMEM_LIMIT_FFN = 56 * MIB
VMEM_LIMIT = 48 * MIB

F32 = jnp.float32
BF16 = jnp.bfloat16
MASK_VALUE = -1e30
MASK_FEATURE = -2.0 ** 100


def _rms(x, g):
    ms = jnp.mean(x * x, axis=-1, keepdims=True)
    return x * lax.rsqrt(ms + EPS) * g


def _twice_gelu_times(x, v):
    c0 = math.sqrt(2.0 / math.pi)
    t = jnp.tanh(x * (c0 + (c0 * 0.044715) * (x * x)))
    return (x * v) * (1.0 + t)


def _const_spec(shape):
    nd = len(shape)
    return pl.BlockSpec(shape, lambda b, j: (0,) * nd, pipeline_mode=pl.Buffered(1))


def _rope_kernel(pos_ref, freq_ref, cos_ref, sin_ref):
    ang = pos_ref[...].astype(F32) * freq_ref[...]
    cos_ref[...] = jnp.cos(ang)
    sin_ref[...] = jnp.sin(ang)


def _rope_tables(positions):
    rows = BATCH * SEQ * ROPE_HALF // LANES
    pos_rep = jnp.repeat(positions.reshape(-1), ROPE_HALF).reshape(rows, LANES)
    inv_freq = 1.0 / (ROPE_THETA ** (jnp.arange(0, QK_ROPE, 2, dtype=F32) / QK_ROPE))
    freq_row = jnp.tile(inv_freq, LANES // ROPE_HALF).reshape(1, LANES)
    cos_d, sin_d = pl.pallas_call(
        _rope_kernel,
        out_shape=(jax.ShapeDtypeStruct((rows, LANES), F32),) * 2,
        name="rope_tables",
    )(pos_rep, freq_row)
    return cos_d.reshape(BATCH, SEQ, ROPE_HALF), sin_d.reshape(BATCH, SEQ, ROPE_HALF)


def _lru_kernel(zero_ref, x_ref, pre_g, post_g, w_in, b_in, conv_w, conv_b, w_ri, b_ri, lam, w_out,
                b_out, o_ref, xb_scr0, xb_scr1, tail_scr, b_scr, gate_scr, y_scr, h_scr):
    j = pl.program_id(1)
    z = zero_ref[0]

    @pl.when(j == 0)
    def _():
        tail_scr[...] = jnp.zeros(tail_scr.shape, F32)
        h_scr[...] = jnp.zeros((1, LRU_WIDTH), F32)

    x = x_ref[0]
    h = _rms(x, pre_g[...]).astype(BF16)
    xb_scrs = (xb_scr0, xb_scr1)

    def stage(hd):
        cols = slice(LRU_WIDTH + hd * LRU_BLOCK, LRU_WIDTH + (hd + 1) * LRU_BLOCK)
        xb_scrs[hd % 2][z, SUBLANES:, :] = (
            jnp.dot(h, w_in[:, cols], preferred_element_type=F32) + b_in[:, cols])

    sp = lam[...]
    neg_c_softplus = -LRU_C * (jnp.maximum(-sp, 0.0) + jnp.log(1.0 + jnp.exp(-jnp.abs(sp))))

    stage(0)
    for hd in range(LRU_HEADS):
        sl = slice(hd * LRU_BLOCK, (hd + 1) * LRU_BLOCK)
        if hd + 1 < LRU_HEADS:
            stage(hd + 1)
        gate_scr[:, sl] = jnp.dot(h, w_in[:, sl], preferred_element_type=F32) + b_in[:, sl]
        scr = xb_scrs[hd % 2]
        scr[z, 0:SUBLANES, :] = tail_scr[hd]
        tail_scr[hd] = scr[z, TM:TM + SUBLANES, :]
        xe = scr[z]
        xc = xe * conv_w[LRU_CONV - 1:LRU_CONV, sl] + conv_b[:, sl]
        for k in range(1, LRU_CONV):
            xc = xc + pltpu.roll(xe, k, 0) * conv_w[LRU_CONV - 1 - k:LRU_CONV - k, sl]
        xh = xc[SUBLANES:, :]
        zz = jnp.dot(xh.astype(BF16), w_ri[hd], preferred_element_type=F32) + b_ri[hd]
        gates = 0.5 * jnp.tanh(0.5 * zz) + 0.5
        r, ig = gates[:, :LRU_BLOCK], gates[:, LRU_BLOCK:]
        log_a = r * neg_c_softplus[:, sl]
        a = jnp.exp(log_a)
        d = 1.0 - a * a
        b = jnp.where(d > 0.0, d * lax.rsqrt(d), 0.0) * (ig * xh)
        row = lax.broadcasted_iota(jnp.int32, (SUBLANES, LRU_BLOCK), 0)
        hprev = h_scr[:, sl]
        for g in range(TM // SUBLANES):
            ag = a[g * SUBLANES:(g + 1) * SUBLANES, :]
            bg = b[g * SUBLANES:(g + 1) * SUBLANES, :]
            for s in (1, 2, 4):
                keep = row >= s
                bg = jnp.where(keep, ag * pltpu.roll(bg, s, 0) + bg, bg)
                ag = jnp.where(keep, ag * pltpu.roll(ag, s, 0), ag)
            hs = ag * hprev + bg
            hprev = hs[SUBLANES - 1:SUBLANES, :]
            b_scr[g * SUBLANES:(g + 1) * SUBLANES, sl] = hs
        h_scr[:, sl] = hprev
        y_scr[:, sl] = _twice_gelu_times(gate_scr[:, sl], b_scr[:, sl]).astype(BF16)

    m = jnp.dot(y_scr[...], w_out[...], preferred_element_type=F32) + b_out[...]
    o_ref[0] = x + _rms(m, post_g[...])


def _lru_mixer(x, pre_g, post_g, w_in, b_in, conv_w, conv_b, w_ri, b_ri, lam, w_out, b_out):
    tok = pl.BlockSpec((1, TM, D_MODEL), lambda b, j: (b, j, 0))
    consts = (pre_g, post_g, w_in, b_in, conv_w, conv_b, w_ri, b_ri, lam, w_out, b_out)
    return pl.pallas_call(
        _lru_kernel,
        grid=(BATCH, SEQ // TM),
        in_specs=[pl.BlockSpec(memory_space=pltpu.SMEM), tok] + [_const_spec(c.shape) for c in consts],
        out_specs=tok,
        out_shape=jax.ShapeDtypeStruct((BATCH, SEQ, D_MODEL), F32),
        scratch_shapes=[
            pltpu.VMEM((1, SUBLANES + TM, LRU_BLOCK), F32),
            pltpu.VMEM((1, SUBLANES + TM, LRU_BLOCK), F32),
            pltpu.VMEM((LRU_HEADS, SUBLANES, LRU_BLOCK), F32),
            pltpu.VMEM((TM, LRU_WIDTH), F32),
            pltpu.VMEM((TM, LRU_WIDTH), F32),
            pltpu.VMEM((TM, LRU_WIDTH), BF16),
            pltpu.VMEM((1, LRU_WIDTH), F32),
        ],
        compiler_params=pltpu.CompilerParams(
            dimension_semantics=("arbitrary", "arbitrary"),
            vmem_limit_bytes=VMEM_LIMIT,
        ),
        name="lru_mixer",
    )(jnp.zeros((1,), jnp.int32), x, *consts)


def _ffn_body(zero_ref, x, pre_g, post_g, w_up, cw, cb, w_dn, o_ref, u_scrs, tail_scr, act_scr):
    j = pl.program_id(1)
    z = zero_ref[0]

    @pl.when(j == 0)
    def _():
        tail_scr[...] = jnp.zeros(tail_scr.shape, F32)

    h = _rms(x, pre_g[...]).astype(BF16)

    def up(c):
        for half, col0 in enumerate((c * FFN_CHUNK, D_FF + c * FFN_CHUNK)):
            u_scrs[c % 2][z, SUBLANES:, half * FFN_CHUNK:(half + 1) * FFN_CHUNK] = jnp.dot(
                h, w_up[:, col0:col0 + FFN_CHUNK], preferred_element_type=F32)

    up(0)
    for c in range(N_FFN_CHUNKS):
        u_scr = u_scrs[c % 2]
        if c + 1 < N_FFN_CHUNKS:
            up(c + 1)
        u_scr[z, 0:SUBLANES, :] = tail_scr[c]
        tail_scr[c] = u_scr[z, TM:TM + SUBLANES, :]
        u = u_scr[z]
        cu = u * cw[c, FFN_CONV - 1:FFN_CONV, :] + cb[c]
        for k in range(1, FFN_CONV):
            cu = cu + pltpu.roll(u, k, 0) * cw[c, FFN_CONV - 1 - k:FFN_CONV - k, :]
        cu = cu[SUBLANES:, :]
        act = _twice_gelu_times(cu[:, :FFN_CHUNK], cu[:, FFN_CHUNK:])
        act_scr[:, c * FFN_CHUNK:(c + 1) * FFN_CHUNK] = act.astype(BF16)
    y = jnp.dot(act_scr[...], w_dn[...], preferred_element_type=F32)
    o_ref[0] = x + _rms(y, post_g[...])


def _ffn_kernel(zero_ref, x_ref, pre_g, post_g, w_up, cw, cb, w_dn, o_ref,
                u_scr0, u_scr1, tail_scr, act_scr):
    _ffn_body(zero_ref, x_ref[0], pre_g, post_g, w_up, cw, cb, w_dn, o_ref, (u_scr0, u_scr1),
              tail_scr, act_scr)


def _proj_ffn_kernel(zero_ref, x_ref, a_ref, w_o, mix_g, pre_g, post_g, w_up, cw, cb, w_dn, o_ref,
                     u_scr0, u_scr1, tail_scr, act_scr):
    m = jnp.dot(a_ref[0], w_o[...], preferred_element_type=F32)
    x = x_ref[0] + _rms(m, mix_g[...])
    _ffn_body(zero_ref, x, pre_g, post_g, w_up, cw, cb, w_dn, o_ref, (u_scr0, u_scr1), tail_scr,
              act_scr)


def _ffn_params(w_up, conv_w, conv_b, w_down):
    def chunked(t):
        g = t[..., :D_FF].reshape(t.shape[:-1] + (N_FFN_CHUNKS, FFN_CHUNK))
        v = t[..., D_FF:].reshape(t.shape[:-1] + (N_FFN_CHUNKS, FFN_CHUNK))
        return jnp.moveaxis(jnp.concatenate([g, v], axis=-1), -2, 0)

    return (w_up.astype(BF16), chunked(conv_w), chunked(conv_b[None, :]),
            (0.5 * w_down).astype(BF16))


def _ffn_call(kernel, token_inputs, consts):
    tok = pl.BlockSpec((1, TM, D_MODEL), lambda b, j: (b, j, 0))
    return pl.pallas_call(
        kernel,
        grid=(BATCH, SEQ // TM),
        in_specs=[pl.BlockSpec(memory_space=pltpu.SMEM)] + [tok] * len(token_inputs)
        + [_const_spec(c.shape) for c in consts],
        out_specs=tok,
        out_shape=jax.ShapeDtypeStruct((BATCH, SEQ, D_MODEL), F32),
        scratch_shapes=[pltpu.VMEM((1, SUBLANES + TM, 2 * FFN_CHUNK), F32),
                        pltpu.VMEM((1, SUBLANES + TM, 2 * FFN_CHUNK), F32),
                        pltpu.VMEM((N_FFN_CHUNKS, SUBLANES, 2 * FFN_CHUNK), F32),
                        pltpu.VMEM((TM, D_FF), BF16)],
        compiler_params=pltpu.CompilerParams(
            dimension_semantics=("arbitrary", "arbitrary"),
            vmem_limit_bytes=VMEM_LIMIT_FFN,
        ),
        name=kernel.__name__.strip("_"),
    )(jnp.zeros((1,), jnp.int32), *token_inputs, *consts)


def _mla_proj_kernel(x_ref, ct_ref, st_ref, pre_g, w_in, q_g, w_q, kv_g, w_k, w_vt, q_feat,
                     q_ref, k_ref, vt_ref):
    h = _rms(x_ref[0], pre_g[...]).astype(BF16)
    c = jnp.dot(h, w_in[...], preferred_element_type=F32)
    qn = _rms(c[:, :Q_LORA], q_g[...]).astype(BF16)
    kvn = _rms(c[:, Q_LORA:Q_LORA + KV_LORA], kv_g[...]).astype(BF16)
    cos, sin = ct_ref[0], st_ref[0]
    ones = jnp.ones((TM, NOPE_A), F32)
    zeros = jnp.zeros((TM, NOPE_A), F32)
    ct = jnp.concatenate([cos, ones, cos, ones], axis=1)
    st = jnp.concatenate([-sin, zeros, sin, zeros], axis=1)

    def rope(t):
        return t * ct + pltpu.roll(t, HEAD_LANES // 2, 1) * st

    kpe = rope(c[:, Q_LORA + KV_LORA:])
    q = jnp.dot(qn, w_q[...], preferred_element_type=F32)
    kn = jnp.dot(kvn, w_k[...], preferred_element_type=F32)
    q_scale = QK_DIM ** -0.5 * math.log2(math.e)
    for hd in range(MLA_HEADS):
        sl = slice(hd * HEAD_LANES, (hd + 1) * HEAD_LANES)
        q_ref[0, hd] = (rope(q[:, sl]) * q_scale + q_feat[...]).astype(BF16)
        k_ref[0, hd] = (kn[:, sl] + kpe).astype(BF16)
    vt = lax.dot_general(w_vt[...], kvn, (((1,), (1,)), ((), ())), preferred_element_type=F32)
    vt_ref[0, :, 0, 0:2 * V_DIM, :] = vt.astype(BF16).reshape(MLA_HEADS // 2, 2 * V_DIM, TM)
    vt_ref[0, :, 0, 2 * V_DIM:, :] = jnp.ones((MLA_HEADS // 2, ONES_ROWS, TM), BF16)


def _head_layout(nope, x1, x2):
    lead = nope.shape[:-1]
    z = lambda n: jnp.zeros(lead + (n,), nope.dtype)
    return jnp.concatenate([
        z(ROPE_HALF) if x1 is None else x1, nope[..., :NOPE_A],
        z(ROPE_HALF) if x2 is None else x2, nope[..., NOPE_A:],
        z(HEAD_LANES - NOPE_B_OFF - (QK_NOPE - NOPE_A)),
    ], axis=-1)


def _mla_params(w_in, w_qb, w_kvb):
    lat = Q_LORA + KV_LORA
    zpad = jnp.zeros((D_MODEL, HEAD_LANES // 2 - ROPE_HALF), w_in.dtype)
    w_in_p = jnp.concatenate([w_in[:, :lat], w_in[:, lat:lat + ROPE_HALF], zpad,
                              w_in[:, lat + ROPE_HALF:], zpad], axis=-1)
    wq = w_qb.reshape(Q_LORA, MLA_HEADS, QK_DIM)
    wq_p = _head_layout(wq[..., :QK_NOPE], wq[..., QK_NOPE:QK_NOPE + ROPE_HALF],
                        wq[..., QK_NOPE + ROPE_HALF:]).reshape(Q_LORA, MLA_HEADS * HEAD_LANES)
    wkv = w_kvb.reshape(KV_LORA, MLA_HEADS, QK_NOPE + V_DIM)
    wk_p = _head_layout(wkv[..., :QK_NOPE], None, None).reshape(KV_LORA, MLA_HEADS * HEAD_LANES)
    wv_t = wkv[..., QK_NOPE:].reshape(KV_LORA, MLA_HEADS * V_DIM).T
    return w_in_p.astype(BF16), wq_p.astype(BF16), wk_p.astype(BF16), wv_t.astype(BF16)


def _mla_proj(x, ctab, stab, pre_g, w_in_p, q_g, wq_p, kv_g, wk_p, wv_t):
    tok = pl.BlockSpec((1, TM, D_MODEL), lambda b, j: (b, j, 0))
    tab = pl.BlockSpec((1, TM, ROPE_HALF), lambda b, j: (b, j, 0))
    head = pl.BlockSpec((1, MLA_HEADS, TM, HEAD_LANES), lambda b, j: (b, 0, j, 0))
    vt_blk = pl.BlockSpec((1, MLA_HEADS // 2, 1, VT_ROWS, TM), lambda b, j: (b, 0, j, 0, 0))
    consts = (pre_g, w_in_p, q_g, wq_p, kv_g, wk_p, wv_t, _mask_features()[0])
    return pl.pallas_call(
        _mla_proj_kernel,
        grid=(BATCH, SEQ // TM),
        in_specs=[tok, tab, tab] + [_const_spec(c.shape) for c in consts],
        out_specs=(head, head, vt_blk),
        out_shape=(jax.ShapeDtypeStruct((BATCH, MLA_HEADS, SEQ, HEAD_LANES), BF16),
                   jax.ShapeDtypeStruct((BATCH, MLA_HEADS, SEQ, HEAD_LANES), BF16),
                   jax.ShapeDtypeStruct((BATCH, MLA_HEADS // 2, SEQ // TM, VT_ROWS, TM), BF16)),
        compiler_params=pltpu.CompilerParams(
            dimension_semantics=("arbitrary", "arbitrary"),
            vmem_limit_bytes=VMEM_LIMIT,
        ),
        name="mla_proj",
    )(x, ctab, stab, *consts)


def _mask_features():
    n_chunks = TQ // CHUNK
    chunk = lax.broadcasted_iota(jnp.int32, (TQ, HEAD_LANES), 0) // CHUNK
    c = lax.broadcasted_iota(jnp.int32, (TQ, HEAD_LANES), 1) - FEAT_OFF
    lane_ok = (c >= 0) & (c < n_chunks)
    q_feat = jnp.where(lane_ok & (chunk < c), 1.0, 0.0).astype(F32)
    k_feat = jnp.where(lane_ok & (chunk == c), MASK_FEATURE, 0.0).astype(BF16)
    return q_feat, jnp.stack([jnp.zeros_like(k_feat), k_feat])


def _attn_kernel(tab_ref, q_ref, k_ref, vt_ref, kf_ref, o_ref, s_scr, acc_scr):
    nt = (((1,), (1,)), ((), ()))

    def scores(hh, t):
        q0 = pl.multiple_of(tab_ref[0, t] * TQ, TQ)
        k0 = pl.multiple_of(tab_ref[1, t] * TQ, TQ)
        k = k_ref[0, hh, pl.ds(k0, TQ), :] + kf_ref[tab_ref[2, t]]
        return lax.dot_general(k, q_ref[0, hh, pl.ds(q0, TQ), :], nt,
                               preferred_element_type=F32)

    def softmax_pv(s, vt, state, first):
        m, acc = state
        m = jnp.where(first, MASK_VALUE, m)
        m_new = jnp.maximum(m, jnp.max(s, axis=0, keepdims=True))
        alpha = jnp.exp2(m - m_new)
        p = jnp.exp2(s - m_new)
        acc = alpha * acc + jnp.dot(vt, p.astype(BF16), preferred_element_type=F32)
        return m_new, acc

    s_scr[0] = scores(0, 0)

    def step(t, state, par):
        qi = tab_ref[0, t]
        first = tab_ref[3, t] == 1
        vt = vt_ref[0, 0, tab_ref[1, t]]
        s_scr[2 + par] = scores(1, t)
        st_a = softmax_pv(s_scr[par], vt, state[0], first)
        s_scr[1 - par] = scores(0, t + 1)
        st_b = softmax_pv(s_scr[2 + par], vt, state[1], first)
        acc_scr[qi, 0] = st_a[1]
        acc_scr[qi, 1] = st_b[1]
        return st_a, st_b

    def steps(i, state):
        for k in range(ATTN_UNROLL):
            state = step(ATTN_UNROLL * i + k, state, k % 2)
        return state

    init = tuple((jnp.full((1, TQ), MASK_VALUE, F32), jnp.zeros((VT_ROWS, TQ), F32))
                 for _ in range(2))
    lax.fori_loop(0, N_ATTN_STEPS // ATTN_UNROLL, steps, init)

    first_head = lax.broadcasted_iota(jnp.int32, (2 * V_DIM, TQ), 0) < V_DIM
    for qi in range(SEQ // TQ):
        num = [acc_scr[qi, hh, 0:2 * V_DIM, :] for hh in range(2)]
        den = [acc_scr[qi, hh, 2 * V_DIM:2 * V_DIM + 1, :] for hh in range(2)]
        o_t = jnp.where(first_head, num[0] / den[0], num[1] / den[1])
        o_ref[0, qi * TQ:(qi + 1) * TQ, :] = o_t.T.astype(BF16)


def _attn_steps():
    steps = [(qi, j, int(j == qi), int(j == 0)) for qi in range(SEQ // TQ) for j in range(qi + 1)]
    steps.append(steps[-1])
    return jnp.asarray(list(zip(*steps)), dtype=jnp.int32)


def _attention(q, k, vt):
    tab, k_feat = _attn_steps(), _mask_features()[1]
    qk = pl.BlockSpec((1, 2, SEQ, HEAD_LANES), lambda b, p: (b, p, 0, 0))
    vts = pl.BlockSpec((1, 1, SEQ // TQ, VT_ROWS, TQ), lambda b, p: (b, p, 0, 0, 0))
    vo = pl.BlockSpec((1, SEQ, 2 * V_DIM), lambda b, p: (b, 0, p))
    return pl.pallas_call(
        _attn_kernel,
        grid=(BATCH, MLA_HEADS // 2),
        in_specs=[pl.BlockSpec(memory_space=pltpu.SMEM), qk, qk, vts, _const_spec(k_feat.shape)],
        out_specs=vo,
        out_shape=jax.ShapeDtypeStruct((BATCH, SEQ, MLA_HEADS * V_DIM), BF16),
        scratch_shapes=[pltpu.VMEM((4, TQ, TQ), F32),
                        pltpu.VMEM((SEQ // TQ, 2, VT_ROWS, TQ), F32)],
        compiler_params=pltpu.CompilerParams(
            dimension_semantics=("arbitrary", "arbitrary"),
            vmem_limit_bytes=VMEM_LIMIT,
        ),
        name="chunk_causal_attention",
    )(tab, q, k, vt, k_feat)


def kernel(x, positions, mix_pre_g, mix_post_g, ffn_pre_g, ffn_post_g, lru_w_in, lru_b_in, lru_conv_w, lru_conv_b, lru_w_r, lru_b_r, lru_w_i, lru_b_i, lru_lambda, lru_w_out, lru_b_out, mla_w_in, mla_q_norm_g, mla_w_qb, mla_kv_norm_g, mla_w_kvb, mla_w_out, ffn_w_up, ffn_conv_w, ffn_conv_b, ffn_w_down):
    row = lambda t: t.reshape(1, -1)

    w_ri = jnp.concatenate([lru_w_r[0], lru_w_i[0]], axis=-1).astype(BF16)
    b_ri = jnp.concatenate([lru_b_r[0], lru_b_i[0]], axis=-1)[:, None, :]
    x = _lru_mixer(x, row(mix_pre_g[0]), row(mix_post_g[0]), lru_w_in[0].astype(BF16), row(lru_b_in[0]),
                   lru_conv_w[0], row(lru_conv_b[0]), w_ri, b_ri, row(lru_lambda[0]),
                   (0.5 * lru_w_out[0]).astype(BF16), row(lru_b_out[0]))
    x = _ffn_call(_ffn_kernel, (x,),
                  (row(ffn_pre_g[0]), row(ffn_post_g[0]))
                  + _ffn_params(ffn_w_up[0], ffn_conv_w[0], ffn_conv_b[0], ffn_w_down[0]))

    ctab, stab = _rope_tables(positions)
    w_in_p, wq_p, wk_p, wv_t = _mla_params(mla_w_in[0], mla_w_qb[0], mla_w_kvb[0])
    q, k, vt = _mla_proj(x, ctab, stab, row(mix_pre_g[1]), w_in_p, row(mla_q_norm_g[0]), wq_p,
                         row(mla_kv_norm_g[0]), wk_p, wv_t)
    a = _attention(q, k, vt)
    x = _ffn_call(_proj_ffn_kernel, (x, a),
                  (mla_w_out[0].astype(BF16), row(mix_post_g[1]), row(ffn_pre_g[1]), row(ffn_post_g[1]))
                  + _ffn_params(ffn_w_up[1], ffn_conv_w[1], ffn_conv_b[1], ffn_w_down[1]))
    return x
```

```python
import math

import jax
import jax.numpy as jnp
from jax import lax
from jax.experimental import pallas as pl
from jax.experimental.pallas import tpu as pltpu

D_MODEL = 1024
BATCH = 4
SEQ = 4096
CHUNK = 64

LRU_WIDTH = 1024
LRU_HEADS = 4
LRU_BLOCK = LRU_WIDTH // LRU_HEADS
LRU_CONV = 4
LRU_C = 8.0

MLA_HEADS = 16
Q_LORA = 768
KV_LORA = 256
QK_NOPE = 64
QK_ROPE = 32
ROPE_HALF = QK_ROPE // 2
V_DIM = 64
QK_DIM = QK_NOPE + QK_ROPE
ROPE_THETA = 10000.0

D_FF = 2816
FFN_CONV = 3
EPS = 1e-6

LANES = 128
SUBLANES = 8
BF16_ROWS = 16

HEAD_LANES = LANES
NOPE_A = 48
X2_OFF = HEAD_LANES // 2
NOPE_B_OFF = X2_OFF + ROPE_HALF
FEAT_OFF = NOPE_B_OFF + (QK_NOPE - NOPE_A)

TM = 512
FFN_CHUNK = 256
N_FFN_CHUNKS = D_FF // FFN_CHUNK
TQ = 512
N_ATTN_STEPS = (SEQ // TQ) * (SEQ // TQ + 1) // 2
ATTN_UNROLL = 12
ONES_ROWS = BF16_ROWS
VT_ROWS = 2 * V_DIM + ONES_ROWS

MIB = 1024 * 1024
VMEM_LIMIT_FFN = 56 * MIB
VMEM_LIMIT = 48 * MIB

F32 = jnp.float32
BF16 = jnp.bfloat16
MASK_VALUE = -1e30
MASK_FEATURE = -2.0 ** 100


def _rms(x, g):
    ms = jnp.mean(x * x, axis=-1, keepdims=True)
    return x * lax.rsqrt(ms + EPS) * g


def _twice_gelu_times(x, v):
    c0 = math.sqrt(2.0 / math.pi)
    t = jnp.tanh(x * (c0 + (c0 * 0.044715) * (x * x)))
    return (x * v) * (1.0 + t)


def _const_spec(shape):
    nd = len(shape)
    return pl.BlockSpec(shape, lambda b, j: (0,) * nd, pipeline_mode=pl.Buffered(1))


def _rope_kernel(pos_ref, freq_ref, cos_ref, sin_ref):
    ang = pos_ref[...].astype(F32) * freq_ref[...]
    cos_ref[...] = jnp.cos(ang)
    sin_ref[...] = jnp.sin(ang)


def _rope_tables(positions):
    rows = BATCH * SEQ * ROPE_HALF // LANES
    pos_rep = jnp.repeat(positions.reshape(-1), ROPE_HALF).reshape(rows, LANES)
    inv_freq = 1.0 / (ROPE_THETA ** (jnp.arange(0, QK_ROPE, 2, dtype=F32) / QK_ROPE))
    freq_row = jnp.tile(inv_freq, LANES // ROPE_HALF).reshape(1, LANES)
    cos_d, sin_d = pl.pallas_call(
        _rope_kernel,
        out_shape=(jax.ShapeDtypeStruct((rows, LANES), F32),) * 2,
        name="rope_tables",
    )(pos_rep, freq_row)
    return cos_d.reshape(BATCH, SEQ, ROPE_HALF), sin_d.reshape(BATCH, SEQ, ROPE_HALF)


def _lru_kernel(zero_ref, x_ref, pre_g, post_g, w_in, b_in, conv_w, conv_b, w_ri, b_ri, lam, w_out,
                b_out, o_ref, xb_scr0, xb_scr1, tail_scr, b_scr, gate_scr, y_scr, h_scr):
    j = pl.program_id(1)
    z = zero_ref[0]

    @pl.when(j == 0)
    def _():
        tail_scr[...] = jnp.zeros(tail_scr.shape, F32)
        h_scr[...] = jnp.zeros((1, LRU_WIDTH), F32)

    x = x_ref[0]
    h = _rms(x, pre_g[...]).astype(BF16)
    xb_scrs = (xb_scr0, xb_scr1)

    def stage(hd):
        cols = slice(LRU_WIDTH + hd * LRU_BLOCK, LRU_WIDTH + (hd + 1) * LRU_BLOCK)
        xb_scrs[hd % 2][z, SUBLANES:, :] = (
            jnp.dot(h, w_in[:, cols], preferred_element_type=F32) + b_in[:, cols])

    sp = lam[...]
    neg_c_softplus = -LRU_C * (jnp.maximum(-sp, 0.0) + jnp.log(1.0 + jnp.exp(-jnp.abs(sp))))

    stage(0)
    for hd in range(LRU_HEADS):
        sl = slice(hd * LRU_BLOCK, (hd + 1) * LRU_BLOCK)
        if hd + 1 < LRU_HEADS:
            stage(hd + 1)
        gate_scr[:, sl] = jnp.dot(h, w_in[:, sl], preferred_element_type=F32) + b_in[:, sl]
        scr = xb_scrs[hd % 2]
        scr[z, 0:SUBLANES, :] = tail_scr[hd]
        tail_scr[hd] = scr[z, TM:TM + SUBLANES, :]
        xe = scr[z]
        xc = xe * conv_w[LRU_CONV - 1:LRU_CONV, sl] + conv_b[:, sl]
        for k in range(1, LRU_CONV):
            xc = xc + pltpu.roll(xe, k, 0) * conv_w[LRU_CONV - 1 - k:LRU_CONV - k, sl]
        xh = xc[SUBLANES:, :]
        zz = jnp.dot(xh.astype(BF16), w_ri[hd], preferred_element_type=F32) + b_ri[hd]
        gates = 0.5 * jnp.tanh(0.5 * zz) + 0.5
        r, ig = gates[:, :LRU_BLOCK], gates[:, LRU_BLOCK:]
        log_a = r * neg_c_softplus[:, sl]
        a = jnp.exp(log_a)
        d = 1.0 - a * a
        b = jnp.where(d > 0.0, d * lax.rsqrt(d), 0.0) * (ig * xh)
        row = lax.broadcasted_iota(jnp.int32, (SUBLANES, LRU_BLOCK), 0)
        hprev = h_scr[:, sl]
        for g in range(TM // SUBLANES):
            ag = a[g * SUBLANES:(g + 1) * SUBLANES, :]
            bg = b[g * SUBLANES:(g + 1) * SUBLANES, :]
            for s in (1, 2, 4):
                keep = row >= s
                bg = jnp.where(keep, ag * pltpu.roll(bg, s, 0) + bg, bg)
                ag = jnp.where(keep, ag * pltpu.roll(ag, s, 0), ag)
            hs = ag * hprev + bg
            hprev = hs[SUBLANES - 1:SUBLANES, :]
            b_scr[g * SUBLANES:(g + 1) * SUBLANES, sl] = hs
        h_scr[:, sl] = hprev
        y_scr[:, sl] = _twice_gelu_times(gate_scr[:, sl], b_scr[:, sl]).astype(BF16)

    m = jnp.dot(y_scr[...], w_out[...], preferred_element_type=F32) + b_out[...]
    o_ref[0] = x + _rms(m, post_g[...])


def _lru_mixer(x, pre_g, post_g, w_in, b_in, conv_w, conv_b, w_ri, b_ri, lam, w_out, b_out):
    tok = pl.BlockSpec((1, TM, D_MODEL), lambda b, j: (b, j, 0))
    consts = (pre_g, post_g, w_in, b_in, conv_w, conv_b, w_ri, b_ri, lam, w_out, b_out)
    return pl.pallas_call(
        _lru_kernel,
        grid=(BATCH, SEQ // TM),
        in_specs=[pl.BlockSpec(memory_space=pltpu.SMEM), tok] + [_const_spec(c.shape) for c in consts],
        out_specs=tok,
        out_shape=jax.ShapeDtypeStruct((BATCH, SEQ, D_MODEL), F32),
        scratch_shapes=[
            pltpu.VMEM((1, SUBLANES + TM, LRU_BLOCK), F32),
            pltpu.VMEM((1, SUBLANES + TM, LRU_BLOCK), F32),
            pltpu.VMEM((LRU_HEADS, SUBLANES, LRU_BLOCK), F32),
            pltpu.VMEM((TM, LRU_WIDTH), F32),
            pltpu.VMEM((TM, LRU_WIDTH), F32),
            pltpu.VMEM((TM, LRU_WIDTH), BF16),
            pltpu.VMEM((1, LRU_WIDTH), F32),
        ],
        compiler_params=pltpu.CompilerParams(
            dimension_semantics=("arbitrary", "arbitrary"),
            vmem_limit_bytes=VMEM_LIMIT,
        ),
        name="lru_mixer",
    )(jnp.zeros((1,), jnp.int32), x, *consts)


def _ffn_body(zero_ref, x, pre_g, post_g, w_up, cw, cb, w_dn, o_ref, u_scrs, tail_scr, act_scr,
              res_scr):
    t = pl.program_id(0)
    cur = t % 2
    prev = 1 - cur
    z = zero_ref[0]

    @pl.when(t == 0)
    def _():
        act_scr[1] = jnp.zeros((TM, D_FF), BF16)
        res_scr[1] = jnp.zeros((TM, D_MODEL), F32)

    @pl.when(t % (SEQ // TM) == 0)
    def _():
        tail_scr[...] = jnp.zeros(tail_scr.shape, F32)

    res_scr[cur] = x
    h = _rms(x, pre_g[...]).astype(BF16)

    y = jnp.dot(act_scr[prev], w_dn[...], preferred_element_type=F32)
    o_ref[0] = res_scr[prev] + _rms(y, post_g[...])

    def up(c):
        for half, col0 in enumerate((c * FFN_CHUNK, D_FF + c * FFN_CHUNK)):
            u_scrs[c % 2][z, SUBLANES:, half * FFN_CHUNK:(half + 1) * FFN_CHUNK] = jnp.dot(
                h, w_up[:, col0:col0 + FFN_CHUNK], preferred_element_type=F32)

    up(0)
    for c in range(N_FFN_CHUNKS):
        u_scr = u_scrs[c % 2]
        if c + 1 < N_FFN_CHUNKS:
            up(c + 1)
        u_scr[z, 0:SUBLANES, :] = tail_scr[c]
        tail_scr[c] = u_scr[z, TM:TM + SUBLANES, :]
        u = u_scr[z]
        cu = u * cw[c, FFN_CONV - 1:FFN_CONV, :] + cb[c]
        for k in range(1, FFN_CONV):
            cu = cu + pltpu.roll(u, k, 0) * cw[c, FFN_CONV - 1 - k:FFN_CONV - k, :]
        cu = cu[SUBLANES:, :]
        act = _twice_gelu_times(cu[:, :FFN_CHUNK], cu[:, FFN_CHUNK:])
        act_scr[cur, :, c * FFN_CHUNK:(c + 1) * FFN_CHUNK] = act.astype(BF16)


def _ffn_kernel(zero_ref, x_ref, pre_g, post_g, w_up, cw, cb, w_dn, o_ref,
                u_scr0, u_scr1, tail_scr, act_scr, res_scr):
    _ffn_body(zero_ref, x_ref[0], pre_g, post_g, w_up, cw, cb, w_dn, o_ref, (u_scr0, u_scr1),
              tail_scr, act_scr, res_scr)


def _proj_ffn_kernel(zero_ref, x_ref, a_ref, w_o, mix_g, pre_g, post_g, w_up, cw, cb, w_dn, o_ref,
                     u_scr0, u_scr1, tail_scr, act_scr, res_scr):
    m = jnp.dot(a_ref[0], w_o[...], preferred_element_type=F32)
    x = x_ref[0] + _rms(m, mix_g[...])
    _ffn_body(zero_ref, x, pre_g, post_g, w_up, cw, cb, w_dn, o_ref, (u_scr0, u_scr1), tail_scr,
              act_scr, res_scr)


def _ffn_params(w_up, conv_w, conv_b, w_down):
    def chunked(t):
        g = t[..., :D_FF].reshape(t.shape[:-1] + (N_FFN_CHUNKS, FFN_CHUNK))
        v = t[..., D_FF:].reshape(t.shape[:-1] + (N_FFN_CHUNKS, FFN_CHUNK))
        return jnp.moveaxis(jnp.concatenate([g, v], axis=-1), -2, 0)

    return (w_up.astype(BF16), chunked(conv_w), chunked(conv_b[None, :]),
            (0.5 * w_down).astype(BF16))


def _ffn_call(kernel, token_inputs, consts):
    tiles = SEQ // TM
    n_tiles = BATCH * tiles

    def tile_spec(lag):
        def index(t):
            tile = jnp.clip(t - lag, 0, n_tiles - 1)
            return tile // tiles, tile % tiles, 0
        return pl.BlockSpec((1, TM, D_MODEL), index)

    def const_spec(shape):
        nd = len(shape)
        return pl.BlockSpec(shape, lambda t: (0,) * nd, pipeline_mode=pl.Buffered(1))

    return pl.pallas_call(
        kernel,
        grid=(n_tiles + 1,),
        in_specs=[pl.BlockSpec(memory_space=pltpu.SMEM)] + [tile_spec(0)] * len(token_inputs)
        + [const_spec(c.shape) for c in consts],
        out_specs=tile_spec(1),
        out_shape=jax.ShapeDtypeStruct((BATCH, SEQ, D_MODEL), F32),
        scratch_shapes=[pltpu.VMEM((1, SUBLANES + TM, 2 * FFN_CHUNK), F32),
                        pltpu.VMEM((1, SUBLANES + TM, 2 * FFN_CHUNK), F32),
                        pltpu.VMEM((N_FFN_CHUNKS, SUBLANES, 2 * FFN_CHUNK), F32),
                        pltpu.VMEM((2, TM, D_FF), BF16),
                        pltpu.VMEM((2, TM, D_MODEL), F32)],
        compiler_params=pltpu.CompilerParams(
            dimension_semantics=("arbitrary",),
            vmem_limit_bytes=VMEM_LIMIT_FFN,
        ),
        name=kernel.__name__.strip("_"),
    )(jnp.zeros((1,), jnp.int32), *token_inputs, *consts)


def _mla_proj_kernel(x_ref, ct_ref, st_ref, pre_g, w_in, q_g, w_q, kv_g, w_k, w_vt, q_feat,
                     q_ref, k_ref, vt_ref):
    h = _rms(x_ref[0], pre_g[...]).astype(BF16)
    c = jnp.dot(h, w_in[...], preferred_element_type=F32)
    qn = _rms(c[:, :Q_LORA], q_g[...]).astype(BF16)
    kvn = _rms(c[:, Q_LORA:Q_LORA + KV_LORA], kv_g[...]).astype(BF16)
    cos, sin = ct_ref[0], st_ref[0]
    ones = jnp.ones((TM, NOPE_A), F32)
    zeros = jnp.zeros((TM, NOPE_A), F32)
    ct = jnp.concatenate([cos, ones, cos, ones], axis=1)
    st = jnp.concatenate([-sin, zeros, sin, zeros], axis=1)

    def rope(t):
        return t * ct + pltpu.roll(t, HEAD_LANES // 2, 1) * st

    kpe = rope(c[:, Q_LORA + KV_LORA:])
    q = jnp.dot(qn, w_q[...], preferred_element_type=F32)
    kn = jnp.dot(kvn, w_k[...], preferred_element_type=F32)
    q_scale = QK_DIM ** -0.5 * math.log2(math.e)
    for hd in range(MLA_HEADS):
        sl = slice(hd * HEAD_LANES, (hd + 1) * HEAD_LANES)
        q_ref[0, hd] = (rope(q[:, sl]) * q_scale + q_feat[...]).astype(BF16)
        k_ref[0, hd] = (kn[:, sl] + kpe).astype(BF16)
    vt = lax.dot_general(w_vt[...], kvn, (((1,), (1,)), ((), ())), preferred_element_type=F32)
    vt_ref[0, :, 0, 0:2 * V_DIM, :] = vt.astype(BF16).reshape(MLA_HEADS // 2, 2 * V_DIM, TM)
    vt_ref[0, :, 0, 2 * V_DIM:, :] = jnp.ones((MLA_HEADS // 2, ONES_ROWS, TM), BF16)


def _head_layout(nope, x1, x2):
    lead = nope.shape[:-1]
    z = lambda n: jnp.zeros(lead + (n,), nope.dtype)
    return jnp.concatenate([
        z(ROPE_HALF) if x1 is None else x1, nope[..., :NOPE_A],
        z(ROPE_HALF) if x2 is None else x2, nope[..., NOPE_A:],
        z(HEAD_LANES - NOPE_B_OFF - (QK_NOPE - NOPE_A)),
    ], axis=-1)


def _mla_params(w_in, w_qb, w_kvb):
    lat = Q_LORA + KV_LORA
    zpad = jnp.zeros((D_MODEL, HEAD_LANES // 2 - ROPE_HALF), w_in.dtype)
    w_in_p = jnp.concatenate([w_in[:, :lat], w_in[:, lat:lat + ROPE_HALF], zpad,
                              w_in[:, lat + ROPE_HALF:], zpad], axis=-1)
    wq = w_qb.reshape(Q_LORA, MLA_HEADS, QK_DIM)
    wq_p = _head_layout(wq[..., :QK_NOPE], wq[..., QK_NOPE:QK_NOPE + ROPE_HALF],
                        wq[..., QK_NOPE + ROPE_HALF:]).reshape(Q_LORA, MLA_HEADS * HEAD_LANES)
    wkv = w_kvb.reshape(KV_LORA, MLA_HEADS, QK_NOPE + V_DIM)
    wk_p = _head_layout(wkv[..., :QK_NOPE], None, None).reshape(KV_LORA, MLA_HEADS * HEAD_LANES)
    wv_t = wkv[..., QK_NOPE:].reshape(KV_LORA, MLA_HEADS * V_DIM).T
    return w_in_p.astype(BF16), wq_p.astype(BF16), wk_p.astype(BF16), wv_t.astype(BF16)


def _mla_proj(x, ctab, stab, pre_g, w_in_p, q_g, wq_p, kv_g, wk_p, wv_t):
    tok = pl.BlockSpec((1, TM, D_MODEL), lambda b, j: (b, j, 0))
    tab = pl.BlockSpec((1, TM, ROPE_HALF), lambda b, j: (b, j, 0))
    head = pl.BlockSpec((1, MLA_HEADS, TM, HEAD_LANES), lambda b, j: (b, 0, j, 0))
    vt_blk = pl.BlockSpec((1, MLA_HEADS // 2, 1, VT_ROWS, TM), lambda b, j: (b, 0, j, 0, 0))
    consts = (pre_g, w_in_p, q_g, wq_p, kv_g, wk_p, wv_t, _mask_features()[0])
    return pl.pallas_call(
        _mla_proj_kernel,
        grid=(BATCH, SEQ // TM),
        in_specs=[tok, tab, tab] + [_const_spec(c.shape) for c in consts],
        out_specs=(head, head, vt_blk),
        out_shape=(jax.ShapeDtypeStruct((BATCH, MLA_HEADS, SEQ, HEAD_LANES), BF16),
                   jax.ShapeDtypeStruct((BATCH, MLA_HEADS, SEQ, HEAD_LANES), BF16),
                   jax.ShapeDtypeStruct((BATCH, MLA_HEADS // 2, SEQ // TM, VT_ROWS, TM), BF16)),
        compiler_params=pltpu.CompilerParams(
            dimension_semantics=("arbitrary", "arbitrary"),
            vmem_limit_bytes=VMEM_LIMIT,
        ),
        name="mla_proj",
    )(x, ctab, stab, *consts)


def _mask_features():
    n_chunks = TQ // CHUNK
    chunk = lax.broadcasted_iota(jnp.int32, (TQ, HEAD_LANES), 0) // CHUNK
    c = lax.broadcasted_iota(jnp.int32, (TQ, HEAD_LANES), 1) - FEAT_OFF
    lane_ok = (c >= 0) & (c < n_chunks)
    q_feat = jnp.where(lane_ok & (chunk < c), 1.0, 0.0).astype(F32)
    k_feat = jnp.where(lane_ok & (chunk == c), MASK_FEATURE, 0.0).astype(BF16)
    return q_feat, jnp.stack([jnp.zeros_like(k_feat), k_feat])


def _attn_kernel(tab_ref, q_ref, k_ref, vt_ref, kf_ref, o_ref, s_scr, acc_scr):
    nt = (((1,), (1,)), ((), ()))

    def scores(hh, t):
        q0 = pl.multiple_of(tab_ref[0, t] * TQ, TQ)
        k0 = pl.multiple_of(tab_ref[1, t] * TQ, TQ)
        k = k_ref[0, hh, pl.ds(k0, TQ), :] + kf_ref[tab_ref[2, t]]
        return lax.dot_general(k, q_ref[0, hh, pl.ds(q0, TQ), :], nt,
                               preferred_element_type=F32)

    def softmax_pv(s, vt, state, first):
        m, acc = state
        m = jnp.where(first, MASK_VALUE, m)
        m_new = jnp.maximum(m, jnp.max(s, axis=0, keepdims=True))
        alpha = jnp.exp2(m - m_new)
        p = jnp.exp2(s - m_new)
        acc = alpha * acc + jnp.dot(vt, p.astype(BF16), preferred_element_type=F32)
        return m_new, acc

    s_scr[0] = scores(0, 0)

    def step(t, state, par):
        qi = tab_ref[0, t]
        first = tab_ref[3, t] == 1
        vt = vt_ref[0, 0, tab_ref[1, t]]
        s_scr[2 + par] = scores(1, t)
        st_a = softmax_pv(s_scr[par], vt, state[0], first)
        s_scr[1 - par] = scores(0, t + 1)
        st_b = softmax_pv(s_scr[2 + par], vt, state[1], first)
        acc_scr[qi, 0] = st_a[1]
        acc_scr[qi, 1] = st_b[1]
        return st_a, st_b

    def steps(i, state):
        for k in range(ATTN_UNROLL):
            state = step(ATTN_UNROLL * i + k, state, k % 2)
        return state

    init = tuple((jnp.full((1, TQ), MASK_VALUE, F32), jnp.zeros((VT_ROWS, TQ), F32))
                 for _ in range(2))
    lax.fori_loop(0, N_ATTN_STEPS // ATTN_UNROLL, steps, init)

    first_head = lax.broadcasted_iota(jnp.int32, (2 * V_DIM, TQ), 0) < V_DIM
    for qi in range(SEQ // TQ):
        num = [acc_scr[qi, hh, 0:2 * V_DIM, :] for hh in range(2)]
        den = [acc_scr[qi, hh, 2 * V_DIM:2 * V_DIM + 1, :] for hh in range(2)]
        o_t = jnp.where(first_head, num[0] / den[0], num[1] / den[1])
        o_ref[0, qi * TQ:(qi + 1) * TQ, :] = o_t.T.astype(BF16)


def _attn_steps():
    steps = [(qi, j, int(j == qi), int(j == 0)) for qi in range(SEQ // TQ) for j in range(qi + 1)]
    steps.append(steps[-1])
    return jnp.asarray(list(zip(*steps)), dtype=jnp.int32)


def _attention(q, k, vt):
    tab, k_feat = _attn_steps(), _mask_features()[1]
    qk = pl.BlockSpec((1, 2, SEQ, HEAD_LANES), lambda b, p: (b, p, 0, 0))
    vts = pl.BlockSpec((1, 1, SEQ // TQ, VT_ROWS, TQ), lambda b, p: (b, p, 0, 0, 0))
    vo = pl.BlockSpec((1, SEQ, 2 * V_DIM), lambda b, p: (b, 0, p))
    return pl.pallas_call(
        _attn_kernel,
        grid=(BATCH, MLA_HEADS // 2),
        in_specs=[pl.BlockSpec(memory_space=pltpu.SMEM), qk, qk, vts, _const_spec(k_feat.shape)],
        out_specs=vo,
        out_shape=jax.ShapeDtypeStruct((BATCH, SEQ, MLA_HEADS * V_DIM), BF16),
        scratch_shapes=[pltpu.VMEM((4, TQ, TQ), F32),
                        pltpu.VMEM((SEQ // TQ, 2, VT_ROWS, TQ), F32)],
        compiler_params=pltpu.CompilerParams(
            dimension_semantics=("arbitrary", "arbitrary"),
            vmem_limit_bytes=VMEM_LIMIT,
        ),
        name="chunk_causal_attention",
    )(tab, q, k, vt, k_feat)


def kernel(x, positions, mix_pre_g, mix_post_g, ffn_pre_g, ffn_post_g, lru_w_in, lru_b_in, lru_conv_w, lru_conv_b, lru_w_r, lru_b_r, lru_w_i, lru_b_i, lru_lambda, lru_w_out, lru_b_out, mla_w_in, mla_q_norm_g, mla_w_qb, mla_kv_norm_g, mla_w_kvb, mla_w_out, ffn_w_up, ffn_conv_w, ffn_conv_b, ffn_w_down):
    row = lambda t: t.reshape(1, -1)

    w_ri = jnp.concatenate([lru_w_r[0], lru_w_i[0]], axis=-1).astype(BF16)
    b_ri = jnp.concatenate([lru_b_r[0], lru_b_i[0]], axis=-1)[:, None, :]
    x = _lru_mixer(x, row(mix_pre_g[0]), row(mix_post_g[0]), lru_w_in[0].astype(BF16), row(lru_b_in[0]),
                   lru_conv_w[0], row(lru_conv_b[0]), w_ri, b_ri, row(lru_lambda[0]),
                   (0.5 * lru_w_out[0]).astype(BF16), row(lru_b_out[0]))
    x = _ffn_call(_ffn_kernel, (x,),
                  (row(ffn_pre_g[0]), row(ffn_post_g[0]))
                  + _ffn_params(ffn_w_up[0], ffn_conv_w[0], ffn_conv_b[0], ffn_w_down[0]))

    ctab, stab = _rope_tables(positions)
    w_in_p, wq_p, wk_p, wv_t = _mla_params(mla_w_in[0], mla_w_qb[0], mla_w_kvb[0])
    q, k, vt = _mla_proj(x, ctab, stab, row(mix_pre_g[1]), w_in_p, row(mla_q_norm_g[0]), wq_p,
                         row(mla_kv_norm_g[0]), wk_p, wv_t)
    a = _attention(q, k, vt)
    x = _ffn_call(_proj_ffn_kernel, (x, a),
                  (mla_w_out[0].astype(BF16), row(mix_post_g[1]), row(ffn_pre_g[1]), row(ffn_post_g[1]))
                  + _ffn_params(ffn_w_up[1], ffn_conv_w[1], ffn_conv_b[1], ffn_w_down[1]))
    return x
```
